```python
import jax
import jax.numpy as jnp
from jax import lax
import numpy as np

D_MODEL = 2048
BATCH = 8
SEQ = 4096
DEPTH = 4

GRID_W = 64
CTX_LEN = 256
N_MIXERS = 4
EPS = 1e-6
ROPE_THETA = 10000.0
F32 = jnp.float32

ML_HEADS = 8
ML_DQK = 128
ML_DV = 256
ML_QK = ML_HEADS * ML_DQK
ML_V = ML_HEADS * ML_DV
ML_CHUNK = 64

NA_HEADS = 16
NA_DH = D_MODEL // NA_HEADS
WIN_ROWS = 8
WIN_COLS = 16

CV_WIDTH = 31

LRU_WIDTH = D_MODEL
LRU_BLOCKS = 8
LRU_BW = LRU_WIDTH // LRU_BLOCKS
LRU_CONV = 4
LRU_C = 8.0

D_FF = 5632
FFN_CONV = 3

kernel_name = 'hybrid_mlstm_natten_conformer_rglru_dit'


def rms_norm(x, g):
    xf = x.astype(F32)
    y = xf * lax.rsqrt(jnp.mean(xf * xf, axis=-1, keepdims=True) + EPS)
    return (y * g.astype(F32)).astype(x.dtype)


def layer_norm(x, g, b):
    xf = x.astype(F32)
    mu = jnp.mean(xf, axis=-1, keepdims=True)
    var = jnp.mean(jnp.square(xf - mu), axis=-1, keepdims=True)
    return ((xf - mu) * lax.rsqrt(var + EPS) * g.astype(F32) + b.astype(F32)).astype(x.dtype)


def modulate(x, shift, scale):
    return x * (1 + scale) + shift


def dwconv(x, w, pad_left, pad_right):
    return lax.conv_general_dilated(
        x, w[:, None, :].astype(x.dtype), window_strides=(1,),
        padding=[(pad_left, pad_right)], dimension_numbers=('NWC', 'WIO', 'NWC'),
        feature_group_count=x.shape[-1])


def rope_1d(x, pos):
    d = x.shape[-1]
    freqs = ROPE_THETA ** (-jnp.arange(0, d, 2, dtype=F32) / d)
    ang = pos.astype(F32)[:, None] * freqs[None, :]
    cos, sin = jnp.cos(ang).astype(x.dtype), jnp.sin(ang).astype(x.dtype)
    x1, x2 = jnp.split(x, 2, axis=-1)
    return jnp.concatenate([x1 * cos - x2 * sin, x1 * sin + x2 * cos], axis=-1)


def axial_rope(x):
    S, d = x.shape[-2], x.shape[-1]
    t = jnp.arange(S)
    return jnp.concatenate([rope_1d(x[..., : d // 2], t // GRID_W),
                            rope_1d(x[..., d // 2:], t % GRID_W)], axis=-1)


def mlstm_chunkwise(q, k, v, logi, logf, state):
    B, H, S, _ = q.shape
    L = ML_CHUNK
    nc = S // L

    def chunks(a):
        return jnp.moveaxis(a.reshape(a.shape[:2] + (nc, L) + a.shape[3:]), 2, 0)

    tril = jnp.tril(jnp.ones((L, L), dtype=bool))

    def step(carry, inp):
        C, n, m = carry
        qc, kc, vc, li, lf = inp
        qc, kc, vc = qc.astype(F32), kc.astype(F32), vc.astype(F32)
        b = jnp.cumsum(lf, axis=-1)
        dmat = jnp.where(tril, b[..., :, None] - b[..., None, :] + li[..., None, :], -jnp.inf)
        inter = b + m[..., None]
        m_t = jnp.maximum(inter, jnp.max(dmat, axis=-1))
        w_inter = jnp.exp(inter - m_t)
        s_qk = jnp.einsum('bhtd,bhsd->bhts', qc, kc) * jnp.exp(dmat - m_t[..., None])
        num = (w_inter[..., None] * jnp.einsum('bhtd,bhde->bhte', qc, C)
               + jnp.einsum('bhts,bhse->bhte', s_qk, vc))
        den = w_inter * jnp.einsum('bhtd,bhd->bht', qc, n) + jnp.sum(s_qk, axis=-1)
        h = num / jnp.maximum(jnp.abs(den), jnp.exp(-m_t))[..., None]
        b_last = b[..., -1]
        w_s = b_last[..., None] - b + li
        m_new = jnp.maximum(b_last + m, jnp.max(w_s, axis=-1))
        decay = jnp.exp(b_last + m - m_new)
        e_s = jnp.exp(w_s - m_new[..., None])
        C_new = decay[..., None, None] * C + jnp.einsum('bhs,bhsd,bhse->bhde', e_s, kc, vc)
        n_new = decay[..., None] * n + jnp.einsum('bhs,bhsd->bhd', e_s, kc)
        return (C_new, n_new, m_new), h

    state, hs = lax.scan(step, state, (chunks(q), chunks(k), chunks(v), chunks(logi), chunks(logf)))
    return jnp.moveaxis(hs, 0, 2).reshape(B, H, S, -1), state


def mlstm_mixer(xl, xc, w_in, w_gate, b_gate, head_g, w_out, need_ctx):
    def project(x, rotary):
        B, S, _ = x.shape
        q, k, v, o = jnp.split(x @ w_in, [ML_QK, 2 * ML_QK, 2 * ML_QK + ML_V], axis=-1)
        q = q.reshape(B, S, ML_HEADS, ML_DQK).transpose(0, 2, 1, 3) * (ML_DQK ** -0.5)
        k = k.reshape(B, S, ML_HEADS, ML_DQK).transpose(0, 2, 1, 3)
        v = v.reshape(B, S, ML_HEADS, ML_DV).transpose(0, 2, 1, 3)
        if rotary:
            q, k = axial_rope(q), axial_rope(k)
        g = (x @ w_gate).astype(F32) + b_gate.astype(F32)
        g = g.reshape(B, S, 4, ML_HEADS).transpose(2, 0, 3, 1)
        fwd = (g[0], jax.nn.log_sigmoid(g[1]))
        bwd = (g[2], jax.nn.log_sigmoid(g[3]))
        return q, k, v, o, fwd, bwd

    def flip(a):
        return jnp.flip(a, axis=2)

    def readout(h, o):
        B, H, S, dv = h.shape
        h = rms_norm(h.transpose(0, 2, 1, 3), head_g.reshape(H, dv)).reshape(B, S, H * dv)
        return ((h.astype(o.dtype) * jax.nn.sigmoid(o)) @ w_out).astype(o.dtype)

    B = xl.shape[0]
    zero = (jnp.zeros((B, ML_HEADS, ML_DQK, ML_DV), F32), jnp.zeros((B, ML_HEADS, ML_DQK), F32),
            jnp.zeros((B, ML_HEADS), F32))
    qc, kc, vc, oc, gfc, gbc = project(xc, False)
    hc_f, st_f = mlstm_chunkwise(qc, kc, vc, gfc[0], gfc[1], zero)
    hc_b, st_b = mlstm_chunkwise(flip(qc), flip(kc), flip(vc), flip(gbc[0]), flip(gbc[1]), zero)
    ql, kl, vl, ol, gfl, gbl = project(xl, True)
    hl_f, _ = mlstm_chunkwise(ql, kl, vl, gfl[0], gfl[1], st_f)
    hl_b, _ = mlstm_chunkwise(flip(ql), flip(kl), flip(vl), flip(gbl[0]), flip(gbl[1]), st_b)
    yl = readout(hl_f + flip(hl_b), ol)
    yc = readout(hc_f + flip(hc_b), oc) if need_ctx else None
    return yl, yc


def na_mixer(xl, xc, w_qkv, q_g, k_g, rpb, w_o, need_ctx):
    def qkv(x):
        B, S, _ = x.shape
        z = (x @ w_qkv).reshape(B, S, 3, NA_HEADS, NA_DH)
        q = rms_norm(z[:, :, 0], q_g) * (NA_DH ** -0.5)
        k = rms_norm(z[:, :, 1], k_g)
        v = z[:, :, 2]
        return q.transpose(0, 2, 1, 3), k.transpose(0, 2, 1, 3), v.transpose(0, 2, 1, 3)

    ql, kl, vl = qkv(xl)
    qc, kc, vc = qkv(xc)
    B, H, S, dh = ql.shape
    rows = S // GRID_W
    win_r = min(WIN_ROWS, rows)
    n_loc = win_r * GRID_W
    kg = kl.reshape(B, H, rows, GRID_W, dh)
    vg = vl.reshape(B, H, rows, GRID_W, dh)
    qrows = jnp.moveaxis(ql.reshape(B, H, rows, GRID_W, dh), 2, 0)
    col = jnp.arange(GRID_W)
    cstart = jnp.clip(col - WIN_COLS // 2, 0, GRID_W - WIN_COLS)
    colmask = (col[None, :] >= cstart[:, None]) & (col[None, :] < cstart[:, None] + WIN_COLS)
    dc_idx = jnp.clip(col[None, :] - col[:, None] + WIN_COLS - 1, 0, 2 * WIN_COLS - 2)

    def row_block(args):
        r, q_r = args
        rs = jnp.clip(r - WIN_ROWS // 2, 0, rows - win_r)
        k_s = lax.dynamic_slice_in_dim(kg, rs, win_r, axis=2)
        v_s = lax.dynamic_slice_in_dim(vg, rs, win_r, axis=2)
        dr_idx = rs + jnp.arange(win_r) - r + WIN_ROWS - 1
        bias = rpb[:, dr_idx[None, :, None], dc_idx[:, None, :]].astype(F32)
        s_loc = jnp.einsum('bhqd,bhrkd->bhqrk', q_r, k_s).astype(F32) + bias
        s_loc = jnp.where(colmask[:, None, :], s_loc, -jnp.inf)
        s_ctx = jnp.einsum('bhqd,bhcd->bhqc', q_r, kc).astype(F32)
        p = jax.nn.softmax(jnp.concatenate([s_loc.reshape(B, H, GRID_W, n_loc), s_ctx], axis=-1),
                           axis=-1).astype(v_s.dtype)
        p_loc = p[..., :n_loc].reshape(B, H, GRID_W, win_r, GRID_W)
        return (jnp.einsum('bhqrk,bhrkd->bhqd', p_loc, v_s)
                + jnp.einsum('bhqc,bhcd->bhqd', p[..., n_loc:], vc))

    o = lax.map(row_block, (jnp.arange(rows), qrows))
    yl = o.transpose(1, 0, 3, 2, 4).reshape(B, S, H * dh) @ w_o
    yc = None
    if need_ctx:
        pc = jax.nn.softmax(jnp.einsum('bhqd,bhkd->bhqk', qc, kc).astype(F32), axis=-1).astype(vc.dtype)
        oc = jnp.einsum('bhqk,bhkd->bhqd', pc, vc)
        yc = oc.transpose(0, 2, 1, 3).reshape(B, qc.shape[2], H * dh) @ w_o
    return yl, yc


def conformer_conv(x, w_pw1, dw, dw_b, ln_g, ln_b, w_pw2):
    a, g = jnp.split(x @ w_pw1, 2, axis=-1)
    h = a * jax.nn.sigmoid(g)
    h = dwconv(h, dw, CV_WIDTH // 2, CV_WIDTH // 2) + dw_b
    h = layer_norm(h, ln_g, ln_b)
    return jax.nn.silu(h) @ w_pw2


def linear_recurrence(a, b, h0):
    b = b.at[:, 0].add(a[:, 0] * h0)

    def combine(left, right):
        a_l, b_l = left
        a_r, b_r = right
        return a_l * a_r, a_r * b_l + b_r

    _, h = lax.associative_scan(combine, (a, b), axis=1)
    return h


def lru_direction(u, w_r, b_r, w_i, b_i, lam, h0):
    B, S, R = u.shape
    ub = u.reshape(B, S, LRU_BLOCKS, LRU_BW)
    r = jax.nn.sigmoid((jnp.einsum('bsnk,nkj->bsnj', ub, w_r).reshape(B, S, R) + b_r).astype(F32))
    i = jax.nn.sigmoid((jnp.einsum('bsnk,nkj->bsnj', ub, w_i).reshape(B, S, R) + b_i).astype(F32))
    log_a = -LRU_C * jax.nn.softplus(-lam.astype(F32)) * r
    a = jnp.exp(log_a)
    b = jnp.sqrt(-jnp.expm1(2.0 * log_a)) * (i * u.astype(F32))
    return linear_recurrence(a, b, h0)


def rglru_mixer(xl, xc, w_in, conv_w, conv_b, w_gate, b_gate, lam, w_out, need_ctx):
    def branches(x):
        gate, u = jnp.split(x @ w_in, 2, axis=-1)
        u = dwconv(u, conv_w, LRU_CONV // 2, LRU_CONV - 1 - LRU_CONV // 2) + conv_b
        return gate, u

    def flip(a):
        return jnp.flip(a, axis=1)

    def out(h, gate):
        return (h.astype(gate.dtype) * jax.nn.gelu(gate)) @ w_out

    pf = (w_gate[0], b_gate[0], w_gate[1], b_gate[1], lam[0])
    pb = (w_gate[2], b_gate[2], w_gate[3], b_gate[3], lam[1])
    gc, uc = branches(xc)
    gl, ul = branches(xl)
    h0 = jnp.zeros((xl.shape[0], LRU_WIDTH), F32)
    hc_f = lru_direction(uc, *pf, h0)
    hc_b = flip(lru_direction(flip(uc), *pb, h0))
    hl_f = lru_direction(ul, *pf, hc_f[:, -1])
    hl_b = flip(lru_direction(flip(ul), *pb, hc_b[:, 0]))
    yl = out(hl_f + hl_b, gl)
    yc = out(hc_f + hc_b, gc) if need_ctx else None
    return yl, yc


def conv_ffn(x, w_gu, conv_w, w_down):
    g, u = jnp.split(x @ w_gu, 2, axis=-1)
    g = dwconv(g, conv_w, FFN_CONV // 2, FFN_CONV // 2)
    return (jax.nn.silu(g) * u) @ w_down


def setup_inputs(seed: int = 0) -> dict:
    key = jax.random.key(seed)
    ks = iter(jax.random.split(key, 48))

    def nrm(shape, scale):
        return jax.random.normal(next(ks), shape, F32) * scale

    D = D_MODEL
    nA, nB, nC, nD = [len(range(kind, DEPTH, N_MIXERS)) for kind in range(N_MIXERS)]
    f_bias = jnp.linspace(3.0, 6.0, ML_HEADS, dtype=F32)
    a0 = jax.random.uniform(next(ks), (nD, 2, LRU_WIDTH), F32, 0.9, 0.999)
    p0 = a0 ** (1.0 / LRU_C)
    return {
        'x': nrm((BATCH, SEQ, D), 1.0),
        'c': nrm((BATCH, D), 1.0),
        'ctx': nrm((BATCH, CTX_LEN, D), 1.0),
        'c_ctx': nrm((D,), 1.0),
        'norm_mix': 1.0 + nrm((DEPTH, D), 0.1),
        'norm_ffn': 1.0 + nrm((DEPTH, D), 0.1),
        'ada_w': nrm((DEPTH, D, 6 * D), 0.5 * D ** -0.5),
        'ada_b': nrm((DEPTH, 6 * D), 0.02),
        'ml_w_in': nrm((nA, D, 2 * ML_QK + 2 * ML_V), D ** -0.5),
        'ml_w_gate': nrm((nA, D, 4 * ML_HEADS), 0.3 * D ** -0.5),
        'ml_b_gate': jnp.concatenate([nrm((nA, ML_HEADS), 0.1), f_bias + nrm((nA, ML_HEADS), 0.1),
                                      nrm((nA, ML_HEADS), 0.1), f_bias + nrm((nA, ML_HEADS), 0.1)], axis=-1),
        'ml_head_g': 1.0 + nrm((nA, ML_V), 0.1),
        'ml_w_out': nrm((nA, ML_V, D), ML_V ** -0.5),
        'na_w_qkv': nrm((nB, D, 3 * D), D ** -0.5),
        'na_q_g': 1.0 + nrm((nB, NA_DH), 0.1),
        'na_k_g': 1.0 + nrm((nB, NA_DH), 0.1),
        'na_rpb': nrm((nB, NA_HEADS, 2 * WIN_ROWS - 1, 2 * WIN_COLS - 1), 0.2),
        'na_w_o': nrm((nB, D, D), D ** -0.5),
        'cv_w_pw1': nrm((nC, D, 2 * D), D ** -0.5),
        'cv_dw': nrm((nC, CV_WIDTH, D), CV_WIDTH ** -0.5),
        'cv_dw_b': nrm((nC, D), 0.02),
        'cv_ln_g': 1.0 + nrm((nC, D), 0.1),
        'cv_ln_b': nrm((nC, D), 0.02),
        'cv_w_pw2': nrm((nC, D, D), D ** -0.5),
        'lr_w_in': nrm((nD, D, 2 * LRU_WIDTH), D ** -0.5),
        'lr_conv': nrm((nD, LRU_CONV, LRU_WIDTH), LRU_CONV ** -0.5),
        'lr_conv_b': nrm((nD, LRU_WIDTH), 0.02),
        'lr_w_gate': nrm((nD, 4, LRU_BLOCKS, LRU_BW, LRU_BW), LRU_BW ** -0.5),
        'lr_b_gate': nrm((nD, 4, LRU_WIDTH), 0.02),
        'lr_lambda': jnp.log(p0) - jnp.log1p(-p0),
        'lr_w_out': nrm((nD, LRU_WIDTH, D), LRU_WIDTH ** -0.5),
        'ffn_w_gu': nrm((DEPTH, D, 2 * D_FF), D ** -0.5),
        'ffn_conv': nrm((DEPTH, FFN_CONV, D_FF), FFN_CONV ** -0.5),
        'ffn_w_down': nrm((DEPTH, D_FF, D), D_FF ** -0.5),
    }


def reference(x, c, ctx, c_ctx, norm_mix, norm_ffn, ada_w, ada_b,
              ml_w_in, ml_w_gate, ml_b_gate, ml_head_g, ml_w_out,
              na_w_qkv, na_q_g, na_k_g, na_rpb, na_w_o,
              cv_w_pw1, cv_dw, cv_dw_b, cv_ln_g, cv_ln_b, cv_w_pw2,
              lr_w_in, lr_conv, lr_conv_b, lr_w_gate, lr_b_gate, lr_lambda, lr_w_out,
              ffn_w_gu, ffn_conv, ffn_w_down):
    xl, xc = x, ctx
    for i in range(DEPTH):
        kind, j = i % N_MIXERS, i // N_MIXERS
        need_ctx = i < DEPTH - 1
        mod_l = jnp.split((jax.nn.silu(c) @ ada_w[i] + ada_b[i])[:, None, :], 6, axis=-1)
        mod_c = jnp.split((jax.nn.silu(c_ctx) @ ada_w[i] + ada_b[i])[None, None, :], 6, axis=-1)
        hl = modulate(rms_norm(xl, norm_mix[i]), mod_l[0], mod_l[1])
        hc = modulate(rms_norm(xc, norm_mix[i]), mod_c[0], mod_c[1])
        if kind == 0:
            yl, yc = mlstm_mixer(hl, hc, ml_w_in[j], ml_w_gate[j], ml_b_gate[j], ml_head_g[j],
                                 ml_w_out[j], need_ctx)
        elif kind == 1:
            yl, yc = na_mixer(hl, hc, na_w_qkv[j], na_q_g[j], na_k_g[j], na_rpb[j], na_w_o[j], need_ctx)
        elif kind == 2:
            cv = (cv_w_pw1[j], cv_dw[j], cv_dw_b[j], cv_ln_g[j], cv_ln_b[j], cv_w_pw2[j])
            yl = conformer_conv(hl, *cv)
            yc = conformer_conv(hc, *cv) if need_ctx else None
        else:
            yl, yc = rglru_mixer(hl, hc, lr_w_in[j], lr_conv[j], lr_conv_b[j], lr_w_gate[j],
                                 lr_b_gate[j], lr_lambda[j], lr_w_out[j], need_ctx)
        xl = xl + mod_l[2] * yl.astype(xl.dtype)
        xl = xl + mod_l[5] * conv_ffn(modulate(rms_norm(xl, norm_ffn[i]), mod_l[3], mod_l[4]),
                                      ffn_w_gu[i], ffn_conv[i], ffn_w_down[i])
        if need_ctx:
            xc = xc + mod_c[2] * yc.astype(xc.dtype)
            xc = xc + mod_c[5] * conv_ffn(modulate(rms_norm(xc, norm_ffn[i]), mod_c[3], mod_c[4]),
                                          ffn_w_gu[i], ffn_conv[i], ffn_w_down[i])
    return xl
```

```python
import functools

import numpy as np
import jax
import jax.numpy as jnp
from jax import lax
from jax.experimental import pallas as pl
from jax.experimental.pallas import tpu as pltpu

F32 = jnp.float32
BF16 = jnp.bfloat16

EPS = 1e-6
ROPE_THETA = 10000.0
GRID_W = 64
ML_HEADS = 8
ML_CHUNK = 256
NA_HEADS = 16
WIN_ROWS = 8
WIN_COLS = 16
NA_QROWS = 4
LRU_BLOCKS = 8
LRU_C = 8.0
NEG = -1e30

V7X_VMEM_LIMIT = 56 * 1024 * 1024
LANE = 128
SUBLANE = 8
MOD_ROWS = 8
ROW_TILE = 512
CONV_TILE = 256
LRU_TILE = 256


def _cp(*sem):
    return pltpu.CompilerParams(dimension_semantics=sem, vmem_limit_bytes=V7X_VMEM_LIMIT)


def _dot(a, b):
    return jnp.dot(a, b, preferred_element_type=F32)


def _dot_nt(a, b):
    return lax.dot_general(a, b, (((1,), (1,)), ((), ())), preferred_element_type=F32)


def _sigmoid(x):
    return 1.0 / (1.0 + jnp.exp(-x))


def _log_sigmoid(x):
    return jnp.minimum(x, 0.0) - jnp.log(1.0 + jnp.exp(-jnp.abs(x)))


def _softplus(x):
    return jnp.maximum(x, 0.0) + jnp.log(1.0 + jnp.exp(-jnp.abs(x)))


def _rms(x, g):
    return x * lax.rsqrt(jnp.mean(x * x, axis=-1, keepdims=True) + EPS) * g


def _norm_mod(x, g, shift, scale):
    return _rms(x, g) * (1.0 + scale) + shift


def _pick(tile, n):
    assert n % tile == 0, (tile, n)
    return tile


def _ada_kernel(c_ref, w_ref, b_ref, o_ref):
    c = c_ref[...]
    s = (c * _sigmoid(c)).astype(BF16)
    o_ref[0] = _dot(s, w_ref[0].astype(BF16)) + b_ref[0]


def ada_modulation(cc, ada_w, ada_b):
    depth, d, n = ada_w.shape
    rows = cc.shape[0]
    tn = _pick(1024 if n % 1024 == 0 else n, n)
    return pl.pallas_call(
        _ada_kernel,
        grid=(depth, n // tn),
        in_specs=[pl.BlockSpec((rows, d), lambda l, j: (0, 0)),
                  pl.BlockSpec((1, d, tn), lambda l, j: (l, 0, j)),
                  pl.BlockSpec((1, 1, tn), lambda l, j: (l, 0, j))],
        out_specs=pl.BlockSpec((1, rows, tn), lambda l, j: (l, 0, j)),
        out_shape=jax.ShapeDtypeStruct((depth, rows, n), F32),
        compiler_params=_cp("parallel", "parallel"),
        name="ada_modulation",
    )(cc, ada_w, ada_b.reshape(depth, 1, n))


def _proj_kernel(x_ref, g_ref, mod_ref, *rest, glu, bias):
    if glu:
        wa_ref, wg_ref, o_ref, h_scr = rest
    elif bias:
        w_ref, b_ref, o_ref, h_scr = rest
    else:
        w_ref, o_ref, h_scr = rest

    @pl.when(pl.program_id(2) == 0)
    def _():
        h = _norm_mod(x_ref[0], g_ref[...], mod_ref[0, 0:1, :], mod_ref[0, 1:2, :])
        h_scr[...] = h.astype(BF16)

    h = h_scr[...]
    if glu:
        o_ref[0] = _dot(h, wa_ref[...]) * _sigmoid(_dot(h, wg_ref[...]))
    elif bias:
        o_ref[0] = _dot(h, w_ref[...]) + b_ref[...]
    else:
        o_ref[0] = _dot(h, w_ref[...])


def norm_proj(x, g, mod, w, *, tm, tn, bias=None, glu=False):
    b, s, d = x.shape
    n = w.shape[1] // 2 if glu else w.shape[1]
    tm, tn = _pick(tm, s), _pick(tn, n)
    per_batch = mod.shape[0] != 1
    in_specs = [pl.BlockSpec((1, tm, d), lambda bi, i, j: (bi, i, 0)),
                pl.BlockSpec((1, d), lambda bi, i, j: (0, 0)),
                pl.BlockSpec((1, MOD_ROWS, d), lambda bi, i, j: (bi if per_batch else 0, 0, 0))]
    args = [x, g.reshape(1, d), mod]
    if glu:
        in_specs += [pl.BlockSpec((d, tn), lambda bi, i, j: (0, j)),
                     pl.BlockSpec((d, tn), lambda bi, i, j: (0, j + n // tn))]
        args += [w, w]
    else:
        in_specs.append(pl.BlockSpec((d, tn), lambda bi, i, j: (0, j)))
        args.append(w)
        if bias is not None:
            in_specs.append(pl.BlockSpec((1, tn), lambda bi, i, j: (0, j)))
            args.append(bias.reshape(1, n))
    return pl.pallas_call(
        functools.partial(_proj_kernel, glu=glu, bias=bias is not None),
        grid=(b, s // tm, n // tn),
        in_specs=in_specs,
        out_specs=pl.BlockSpec((1, tm, tn), lambda bi, i, j: (bi, i, j)),
        out_shape=jax.ShapeDtypeStruct((b, s, n), F32),
        scratch_shapes=[pltpu.VMEM((tm, d), BF16)],
        compiler_params=_cp("parallel", "parallel", "arbitrary"),
        name="norm_proj",
    )(*args)


def _resid_kernel(*refs, prologue, n_in, gate_row):
    in_refs = refs[:n_in]
    w_ref, x_ref, mod_ref, o_ref, h_scr, *extra_scr = refs[n_in:]
    pos = (pl.program_id(1), pl.num_programs(1))

    @pl.when(pl.program_id(2) == 0)
    def _():
        prologue(pos, *in_refs, h_scr, *extra_scr)

    y = _dot(h_scr[...], w_ref[...])
    o_ref[0] = x_ref[0] + mod_ref[0, gate_row:gate_row + 1, :] * y


def resid_proj(prologue, ins, in_specs, k, w, x, mod, *, gate_row, tm, tn, scratch=()):
    b, s, d = x.shape
    n = w.shape[1]
    per_batch = mod.shape[0] != 1
    specs = list(in_specs) + [
        pl.BlockSpec((k, tn), lambda bi, i, j: (0, j)),
        pl.BlockSpec((1, tm, tn), lambda bi, i, j: (bi, i, j)),
        pl.BlockSpec((1, MOD_ROWS, tn), lambda bi, i, j: (bi if per_batch else 0, 0, j))]
    return pl.pallas_call(
        functools.partial(_resid_kernel, prologue=prologue, n_in=len(ins), gate_row=gate_row),
        grid=(b, s // tm, n // tn),
        in_specs=specs,
        out_specs=pl.BlockSpec((1, tm, tn), lambda bi, i, j: (bi, i, j)),
        out_shape=jax.ShapeDtypeStruct((b, s, n), F32),
        scratch_shapes=[pltpu.VMEM((tm, k), BF16)] + list(scratch),
        compiler_params=_cp("parallel", "parallel", "arbitrary"),
        name="resid_proj",
    )(*ins, w, x, mod)


def _ffn_kernel(x_ref, xp_ref, xn_ref, g_ref, mod_ref, wg_ref, wu_ref, cw_ref, wd_ref, o_ref,
                h_scr, hh_scr, g_scr, acc_scr, *, tm):
    i, j = pl.program_id(1), pl.program_id(2)
    nt, nf = pl.num_programs(1), pl.num_programs(2)
    hb = SUBLANE

    @pl.when(j == 0)
    def _():
        g, shift, scale = g_ref[...], mod_ref[0, 3:4, :], mod_ref[0, 4:5, :]
        h_scr[...] = _norm_mod(x_ref[0], g, shift, scale).astype(BF16)
        halo = jnp.concatenate([xp_ref[0], xn_ref[0]], axis=0)
        hh_scr[...] = _norm_mod(halo, g, shift, scale).astype(BF16)
        acc_scr[...] = jnp.zeros_like(acc_scr)

    gg = _dot(h_scr[...], wg_ref[...])
    gh = _dot(hh_scr[...], wg_ref[...])
    uu = _dot(h_scr[...], wu_ref[...])
    g_scr[0:hb] = jnp.where(i > 0, gh[0:hb], 0.0)
    g_scr[hb:hb + tm] = gg
    g_scr[hb + tm:] = jnp.where(i < nt - 1, gh[hb:], 0.0)
    cw = cw_ref[...]
    gc = (cw[0:1] * g_scr[pl.ds(hb - 1, tm), :] + cw[1:2] * gg
          + cw[2:3] * g_scr[pl.ds(hb + 1, tm), :])
    act = (gc * _sigmoid(gc) * uu).astype(BF16)
    acc_scr[...] += _dot(act, wd_ref[...])

    @pl.when(j == nf - 1)
    def _():
        o_ref[0] = x_ref[0] + mod_ref[0, 5:6, :] * acc_scr[...]


def conv_ffn(x, g, mod, w_gu, conv_w, w_down, *, tm, fc):
    b, s, d = x.shape
    f = w_down.shape[0]
    tm, fc = _pick(tm, s), _pick(fc, f)
    per_batch = mod.shape[0] != 1
    nh = tm // SUBLANE
    last_h = s // SUBLANE - 1
    return pl.pallas_call(
        functools.partial(_ffn_kernel, tm=tm),
        grid=(b, s // tm, f // fc),
        in_specs=[
            pl.BlockSpec((1, tm, d), lambda bi, i, j: (bi, i, 0)),
            pl.BlockSpec((1, SUBLANE, d), lambda bi, i, j: (bi, jnp.maximum(i * nh - 1, 0), 0)),
            pl.BlockSpec((1, SUBLANE, d), lambda bi, i, j: (bi, jnp.minimum((i + 1) * nh, last_h), 0)),
            pl.BlockSpec((1, d), lambda bi, i, j: (0, 0)),
            pl.BlockSpec((1, MOD_ROWS, d), lambda bi, i, j: (bi if per_batch else 0, 0, 0)),
            pl.BlockSpec((d, fc), lambda bi, i, j: (0, j)),
            pl.BlockSpec((d, fc), lambda bi, i, j: (0, j + f // fc)),
            pl.BlockSpec((conv_w.shape[0], fc), lambda bi, i, j: (0, j)),
            pl.BlockSpec((fc, d), lambda bi, i, j: (j, 0)),
        ],
        out_specs=pl.BlockSpec((1, tm, d), lambda bi, i, j: (bi, i, 0)),
        out_shape=jax.ShapeDtypeStruct((b, s, d), F32),
        scratch_shapes=[pltpu.VMEM((tm, d), BF16),
                        pltpu.VMEM((2 * SUBLANE, d), BF16),
                        pltpu.VMEM((tm + 2 * SUBLANE, fc), F32),
                        pltpu.VMEM((tm, d), F32)],
        compiler_params=_cp("parallel", "parallel", "arbitrary"),
        name="conv_ffn",
    )(x, x, x, g.reshape(1, d), mod, w_gu, w_gu, conv_w, w_down)


def _split3(x):
    hi = x.astype(BF16)
    r = x - hi.astype(F32)
    mid = r.astype(BF16)
    lo = (r - mid.astype(F32)).astype(BF16)
    return hi, mid, lo


def _rope(x, cos, sin_signed):
    half = x.shape[-1] // 4
    lane = lax.broadcasted_iota(jnp.int32, x.shape, 1)
    partner = jnp.where((lane % (2 * half)) < half,
                        pltpu.roll(x, x.shape[-1] - half, 1), pltpu.roll(x, half, 1))
    return x * cos + partner * sin_signed


def _mlstm_kernel(q_ref, k_ref, v_ref, gc_ref, gr_ref, cos_ref, sin_ref, c0_ref, n0_ref, m0_ref,
                  h_ref, c_out, n_out, m_out, c_scr, n_scr, m_scr, *, reverse, rotary, heads):
    hd, c = pl.program_id(1), pl.program_id(2)
    nc = pl.num_programs(2)
    L = q_ref.shape[1]
    dqk = q_ref.shape[2]

    @pl.when(c == 0)
    def _():
        c_scr[...] = c0_ref[0, 0]
        n_scr[...] = n0_ref[0, 0]
        m_scr[...] = m0_ref[0, 0]

    q = q_ref[0] * (dqk ** -0.5)
    k = k_ref[0]
    if rotary:
        q = _rope(q, cos_ref[...], sin_ref[...])
        k = _rope(k, cos_ref[...], sin_ref[...])
    qb, kb, vb = q.astype(BF16), k.astype(BF16), v_ref[0].astype(BF16)

    off_i = 2 * heads if reverse else 0
    off_f = off_i + heads
    gcol = gc_ref[0]
    lane = lax.broadcasted_iota(jnp.int32, gcol.shape, 1)
    li_col = jnp.sum(jnp.where(lane == off_i + hd, gcol, 0.0), axis=-1, keepdims=True)
    li_row = gr_ref[0, pl.ds(off_i + hd, 1), :]
    lf_row = _log_sigmoid(gr_ref[0, pl.ds(off_f + hd, 1), :])
    lf_all = _log_sigmoid(gcol)

    row = lax.broadcasted_iota(jnp.int32, (L, L), 0)
    col = lax.broadcasted_iota(jnp.int32, (L, L), 1)
    mask = (col >= row) if reverse else (col <= row)
    mask_t = (row >= col) if reverse else (row <= col)
    tri = jnp.where(mask, 1.0, 0.0).astype(BF16)
    tri_t = jnp.where(mask_t, 1.0, 0.0).astype(BF16)
    b_all = sum(_dot(tri, p) for p in _split3(lf_all))
    b_col = jnp.sum(jnp.where(lane == off_f + hd, b_all, 0.0), axis=-1, keepdims=True)
    lf_row8 = jnp.broadcast_to(lf_row, (SUBLANE, L))
    b_row = sum(_dot(p, tri_t) for p in _split3(lf_row8))[0:1]
    total = jnp.sum(lf_row, axis=-1, keepdims=True)

    m_prev = m_scr[0:1, 0:1]
    dmat = jnp.where(mask, b_col - b_row + li_row, NEG)
    inter = b_col + m_prev
    m_t = jnp.maximum(inter, jnp.max(dmat, axis=-1, keepdims=True))
    w_inter = jnp.exp(inter - m_t)
    s_qk = _dot_nt(qb, kb) * jnp.exp(dmat - m_t)
    num = w_inter * _dot(qb, c_scr[...].astype(BF16)) + _dot(s_qk.astype(BF16), vb)
    den = (w_inter * jnp.sum(q * n_scr[...], axis=-1, keepdims=True)
           + jnp.sum(s_qk, axis=-1, keepdims=True))
    h_ref[0] = num / jnp.maximum(jnp.abs(den), jnp.exp(-m_t))

    w_col = total - b_col + li_col
    m_new = jnp.maximum(total + m_prev, jnp.max(w_col, axis=0, keepdims=True))
    decay = jnp.exp(total + m_prev - m_new)
    ek = k * jnp.exp(w_col - m_new)
    c_scr[...] = decay * c_scr[...] + _dot(ek.T.astype(BF16), vb)
    n_scr[...] = decay * n_scr[...] + jnp.sum(ek, axis=0, keepdims=True)
    m_scr[...] = jnp.broadcast_to(m_new, m_scr.shape)

    @pl.when(c == nc - 1)
    def _():
        c_out[0, 0] = c_scr[...]
        n_out[0, 0] = n_scr[...]
        m_out[0, 0] = m_scr[...]


def mlstm_scan(z, gates_t, cos, sin, state, *, reverse, rotary, heads, dqk, dv, chunk):
    b, s, _ = z.shape
    L = _pick(chunk, s)
    nc = s // L
    gate_blk = (2 * heads * dqk + 2 * heads * dv) // LANE
    v_blk0 = 2 * heads * dqk // dv

    def ci(c):
        return nc - 1 - c if reverse else c

    c0, n0, m0 = state
    st_specs = [pl.BlockSpec((1, 1, dqk, dv), lambda bi, h, c: (bi, h, 0, 0)),
                pl.BlockSpec((1, 1, 1, dqk), lambda bi, h, c: (bi, h, 0, 0)),
                pl.BlockSpec((1, 1, 1, LANE), lambda bi, h, c: (bi, h, 0, 0))]
    h, c1, n1, m1 = pl.pallas_call(
        functools.partial(_mlstm_kernel, reverse=reverse, rotary=rotary, heads=heads),
        grid=(b, heads, nc),
        in_specs=[
            pl.BlockSpec((1, L, dqk), lambda bi, h, c: (bi, ci(c), h)),
            pl.BlockSpec((1, L, dqk), lambda bi, h, c: (bi, ci(c), heads + h)),
            pl.BlockSpec((1, L, dv), lambda bi, h, c: (bi, ci(c), v_blk0 + h)),
            pl.BlockSpec((1, L, LANE), lambda bi, h, c: (bi, ci(c), gate_blk)),
            pl.BlockSpec((1, 4 * heads, L), lambda bi, h, c: (bi, 0, ci(c))),
            pl.BlockSpec((L, dqk), lambda bi, h, c: (ci(c), 0)),
            pl.BlockSpec((L, dqk), lambda bi, h, c: (ci(c), 0)),
        ] + st_specs,
        out_specs=[pl.BlockSpec((1, L, dv), lambda bi, h, c: (bi, ci(c), h))] + st_specs,
        out_shape=[jax.ShapeDtypeStruct((b, s, heads * dv), F32),
                   jax.ShapeDtypeStruct(c0.shape, F32),
                   jax.ShapeDtypeStruct(n0.shape, F32),
                   jax.ShapeDtypeStruct(m0.shape, F32)],
        scratch_shapes=[pltpu.VMEM((dqk, dv), F32), pltpu.VMEM((1, dqk), F32), pltpu.VMEM((1, LANE), F32)],
        compiler_params=_cp("parallel", "parallel", "arbitrary"),
        name="mlstm_scan",
    )(z, z, z, z, gates_t, cos, sin, c0, n0, m0)
    return h, (c1, n1, m1)


def _rope_tables(s, d):
    half, quarter = d // 2, d // 4
    freqs = ROPE_THETA ** (-jnp.arange(0, half, 2, dtype=F32) / half)
    t = jnp.arange(s)
    ang_r = (t // GRID_W).astype(F32)[:, None] * freqs[None, :]
    ang_c = (t % GRID_W).astype(F32)[:, None] * freqs[None, :]
    cos = jnp.concatenate([jnp.cos(ang_r)] * 2 + [jnp.cos(ang_c)] * 2, axis=-1)
    sin = jnp.concatenate([-jnp.sin(ang_r), jnp.sin(ang_r), -jnp.sin(ang_c), jnp.sin(ang_c)], axis=-1)
    del quarter
    return cos, sin


def _mlstm_readout_prologue(pos, hf_ref, hb_ref, o_ref, g_ref, h_scr, *, heads, dv):
    for hd in range(heads):
        sl = slice(hd * dv, (hd + 1) * dv)
        hs = hf_ref[0, :, sl] + hb_ref[0, :, sl]
        y = _rms(hs, g_ref[:, sl]) * _sigmoid(o_ref[0, :, sl])
        h_scr[:, sl] = y.astype(BF16)


def _na_kernel(q_ref, k_ref, v_ref, kc_ref, vc_ref, qg_ref, kg_ref, bias_ref, o_ref,
               kn_scr, vn_scr, kcn_scr, vcn_scr, *, nkeys, kstep, half, kr_max):
    rb = pl.program_id(2)
    nb = pl.num_programs(2)
    tq, dh = q_ref.shape[1], q_ref.shape[2]

    @pl.when(rb == 0)
    def _():
        kn_scr[...] = _rms(k_ref[0], kg_ref[...]).astype(BF16)
        vn_scr[...] = v_ref[0].astype(BF16)
        kcn_scr[...] = _rms(kc_ref[0], kg_ref[...]).astype(BF16)
        vcn_scr[...] = vc_ref[0].astype(BF16)

    qn = (_rms(q_ref[0], qg_ref[...]) * (dh ** -0.5)).astype(BF16)
    kr0 = jnp.clip(rb * (tq // kstep) - half, 0, kr_max)
    start = pl.multiple_of(kr0 * kstep, kstep)
    cls = jnp.where(rb == 0, 0, jnp.where(rb == nb - 1, 2, 1))
    s_loc = _dot_nt(qn, kn_scr[pl.ds(start, nkeys), :]) + bias_ref[0, cls]
    s_ctx = _dot_nt(qn, kcn_scr[...])
    m = jnp.maximum(jnp.max(s_loc, axis=-1, keepdims=True), jnp.max(s_ctx, axis=-1, keepdims=True))
    p_loc = jnp.exp(s_loc - m)
    p_ctx = jnp.exp(s_ctx - m)
    l = jnp.sum(p_loc, axis=-1, keepdims=True) + jnp.sum(p_ctx, axis=-1, keepdims=True)
    o = _dot(p_loc.astype(BF16), vn_scr[pl.ds(start, nkeys), :]) + _dot(p_ctx.astype(BF16), vcn_scr[...])
    o_ref[0] = o / l


def _na_bias_tables(rows):
    W, R, half = GRID_W, NA_QROWS, WIN_ROWS // 2
    win_r = min(WIN_ROWS, rows)
    U = R + win_r - 1
    nb = rows // R
    kr_max = rows - U
    assert nb >= 3 and R >= half and kr_max >= 0
    ri, c = np.arange(R)[:, None, None, None], np.arange(W)[None, :, None, None]
    kj, kc = np.arange(U)[None, None, :, None], np.arange(W)[None, None, None, :]
    cstart = np.clip(c - WIN_COLS // 2, 0, W - WIN_COLS)
    col_ok = (kc >= cstart) & (kc < cstart + WIN_COLS)
    dc = np.broadcast_to(np.clip(kc - c + WIN_COLS - 1, 0, 2 * WIN_COLS - 2), (R, W, U, W))
    per_block = []
    for rb in range(nb):
        r = rb * R + ri
        kr = np.clip(rb * R - half, 0, kr_max) + kj
        rs = np.clip(r - half, 0, rows - win_r)
        ok = np.broadcast_to((kr >= rs) & (kr < rs + win_r) & col_ok, (R, W, U, W))
        dr = np.broadcast_to(np.clip(kr - r + WIN_ROWS - 1, 0, 2 * WIN_ROWS - 2), (R, W, U, W))
        per_block.append((ok.reshape(R * W, U * W), dr.reshape(R * W, U * W)))
    for rb in range(2, nb - 1):
        assert all(np.array_equal(a, b_) for a, b_ in zip(per_block[1], per_block[rb]))
    sel = [per_block[0], per_block[1], per_block[nb - 1]]
    ok = np.stack([s_[0] for s_ in sel])
    dr = np.stack([s_[1] for s_ in sel])
    dcs = np.stack([dc.reshape(R * W, U * W)] * 3)
    return ok, dr, dcs, U, kr_max


def na_attention(z, zc, q_g, k_g, rpb, *, heads):
    b, s, n3 = z.shape
    dh = n3 // (3 * heads)
    cl = zc.shape[1]
    rows = s // GRID_W
    ok, dr, dc, U, kr_max = _na_bias_tables(rows)
    bias = jnp.where(ok[None], rpb[:, dr, dc], NEG).astype(F32)
    tq, nkeys = NA_QROWS * GRID_W, U * GRID_W
    return pl.pallas_call(
        functools.partial(_na_kernel, nkeys=nkeys, kstep=GRID_W, half=WIN_ROWS // 2, kr_max=kr_max),
        grid=(b, heads, s // tq),
        in_specs=[
            pl.BlockSpec((1, tq, dh), lambda bi, h, r: (bi, r, h)),
            pl.BlockSpec((1, s, dh), lambda bi, h, r: (bi, 0, heads + h)),
            pl.BlockSpec((1, s, dh), lambda bi, h, r: (bi, 0, 2 * heads + h)),
            pl.BlockSpec((1, cl, dh), lambda bi, h, r: (bi, 0, heads + h)),
            pl.BlockSpec((1, cl, dh), lambda bi, h, r: (bi, 0, 2 * heads + h)),
            pl.BlockSpec((1, dh), lambda bi, h, r: (0, 0)),
            pl.BlockSpec((1, dh), lambda bi, h, r: (0, 0)),
            pl.BlockSpec((1, 3, tq, nkeys), lambda bi, h, r: (h, 0, 0, 0)),
        ],
        out_specs=pl.BlockSpec((1, tq, dh), lambda bi, h, r: (bi, r, h)),
        out_shape=jax.ShapeDtypeStruct((b, s, heads * dh), F32),
        scratch_shapes=[pltpu.VMEM((s, dh), BF16), pltpu.VMEM((s, dh), BF16),
                        pltpu.VMEM((cl, dh), BF16), pltpu.VMEM((cl, dh), BF16)],
        compiler_params=_cp("parallel", "parallel", "arbitrary"),
        name="na_attention",
    )(z, z, z, zc, zc, q_g.reshape(1, dh), k_g.reshape(1, dh), bias)


def _ctx_attn_kernel(q_ref, k_ref, v_ref, qg_ref, kg_ref, o_ref):
    dh = q_ref.shape[2]
    qn = (_rms(q_ref[0], qg_ref[...]) * (dh ** -0.5)).astype(BF16)
    kn = _rms(k_ref[0], kg_ref[...]).astype(BF16)
    s = _dot_nt(qn, kn)
    p = jnp.exp(s - jnp.max(s, axis=-1, keepdims=True))
    l = jnp.sum(p, axis=-1, keepdims=True)
    o_ref[0] = _dot(p.astype(BF16), v_ref[0].astype(BF16)) / l


def ctx_attention(zc, q_g, k_g, *, heads):
    b, cl, n3 = zc.shape
    dh = n3 // (3 * heads)
    return pl.pallas_call(
        _ctx_attn_kernel,
        grid=(b, heads),
        in_specs=[pl.BlockSpec((1, cl, dh), lambda bi, h: (bi, 0, h)),
                  pl.BlockSpec((1, cl, dh), lambda bi, h: (bi, 0, heads + h)),
                  pl.BlockSpec((1, cl, dh), lambda bi, h: (bi, 0, 2 * heads + h)),
                  pl.BlockSpec((1, dh), lambda bi, h: (0, 0)),
                  pl.BlockSpec((1, dh), lambda bi, h: (0, 0))],
        out_specs=pl.BlockSpec((1, cl, dh), lambda bi, h: (bi, 0, h)),
        out_shape=jax.ShapeDtypeStruct((b, cl, heads * dh), F32),
        compiler_params=_cp("parallel", "parallel"),
        name="ctx_attention",
    )(zc, zc, zc, q_g.reshape(1, dh), k_g.reshape(1, dh))


def _cast_prologue(pos, a_ref, h_scr):
    h_scr[...] = a_ref[0].astype(BF16)


def _conformer_prologue(pos, a_ref, ap_ref, an_ref, dw_ref, dwb_ref, lg_ref, lb_ref, h_scr, ext_scr, y_scr,
                        *, tm, width, halo):
    i, nt = pos
    d = a_ref.shape[2]
    pad = width // 2
    ext_scr[0:halo] = jnp.where(i > 0, ap_ref[0], 0.0)
    ext_scr[halo:halo + tm] = a_ref[0]
    ext_scr[halo + tm:] = jnp.where(i < nt - 1, an_ref[0], 0.0)

    def col_block(cb, carry):
        cs = pl.ds(pl.multiple_of(cb * LANE, LANE), LANE)
        acc = jnp.broadcast_to(dwb_ref[:, cs], (tm, LANE))
        for k in range(width):
            acc = acc + dw_ref[k:k + 1, cs] * ext_scr[pl.ds(halo - pad + k, tm), cs]
        y_scr[:, cs] = acc
        return carry

    lax.fori_loop(0, d // LANE, col_block, 0)
    y = y_scr[...]
    mu = jnp.mean(y, axis=-1, keepdims=True)
    var = jnp.mean(jnp.square(y - mu), axis=-1, keepdims=True)
    yn = (y - mu) * lax.rsqrt(var + EPS) * lg_ref[...] + lb_ref[...]
    h_scr[...] = (yn * _sigmoid(yn)).astype(BF16)


def _halo_specs(tm, halo, s, width_cols, col_blk=0):
    nh = tm // halo
    last = s // halo - 1
    return [pl.BlockSpec((1, tm, width_cols), lambda bi, i, j: (bi, i, col_blk)),
            pl.BlockSpec((1, halo, width_cols), lambda bi, i, j: (bi, jnp.maximum(i * nh - 1, 0), col_blk)),
            pl.BlockSpec((1, halo, width_cols), lambda bi, i, j: (bi, jnp.minimum((i + 1) * nh, last), col_blk))]


def _lru_kernel(u_ref, up_ref, un_ref, cw_ref, cb_ref, wr_ref, wi_ref, br_ref, bi_ref, lam_ref, h0_ref,
                o_ref, ext_scr, a_scr, b_scr, carry_scr, *, tm, reverse, blocks, halo):
    i, nt = pl.program_id(1), pl.num_programs(1)
    d = u_ref.shape[2]
    bw = d // blocks
    first = (i == 0)
    ti = (nt - 1 - i) if reverse else i

    @pl.when(first)
    def _():
        carry_scr[...] = h0_ref[0]

    ext_scr[0:halo] = jnp.where(ti > 0, up_ref[0], 0.0)
    ext_scr[halo:halo + tm] = u_ref[0]
    ext_scr[halo + tm:] = jnp.where(ti < nt - 1, un_ref[0], 0.0)
    width = cw_ref.shape[0]
    pad_l = width // 2
    u = jnp.broadcast_to(cb_ref[...], (tm, d))
    for k in range(width):
        u = u + cw_ref[k:k + 1, :] * ext_scr[pl.ds(halo - pad_l + k, tm), :]

    ub = u.astype(BF16)
    neg_c_sp = -LRU_C * _softplus(-lam_ref[...])
    for blk in range(blocks):
        sl = slice(blk * bw, (blk + 1) * bw)
        r = _sigmoid(_dot(ub[:, sl], wr_ref[blk]) + br_ref[:, sl])
        g = _sigmoid(_dot(ub[:, sl], wi_ref[blk]) + bi_ref[:, sl])
        log_a = neg_c_sp[:, sl] * r
        a = jnp.exp(log_a)
        a_scr[:, sl] = a
        b_scr[:, sl] = jnp.sqrt(1.0 - a * a) * (g * u[:, sl])

    sub = lax.broadcasted_iota(jnp.int32, (SUBLANE, d), 0)
    ngroups = tm // SUBLANE

    def group(gi, carry):
        r0 = pl.multiple_of(((ngroups - 1 - gi) if reverse else gi) * SUBLANE, SUBLANE)
        a8 = a_scr[pl.ds(r0, SUBLANE), :]
        b8 = b_scr[pl.ds(r0, SUBLANE), :]
        for sft in (1, 2, 4):
            if reverse:
                valid = sub < SUBLANE - sft
                a_sh = pltpu.roll(a8, SUBLANE - sft, 0)
                b_sh = pltpu.roll(b8, SUBLANE - sft, 0)
            else:
                valid = sub >= sft
                a_sh = pltpu.roll(a8, sft, 0)
                b_sh = pltpu.roll(b8, sft, 0)
            b8 = jnp.where(valid, a8 * b_sh + b8, b8)
            a8 = jnp.where(valid, a8 * a_sh, a8)
        h8 = a8 * carry + b8
        o_ref[0, pl.ds(r0, SUBLANE), :] = h8
        return h8[0:1, :] if reverse else h8[SUBLANE - 1:SUBLANE, :]

    carry_scr[...] = lax.fori_loop(0, ngroups, group, carry_scr[...])


def lru_scan(z, h0, conv_w, conv_b, w_r, w_i, b_r, b_i, lam, *, reverse, tm):
    b, s, r2 = z.shape
    r = r2 // 2
    tm = _pick(tm, s)
    nt = s // tm
    halo = SUBLANE
    nh = tm // halo
    last = s // halo - 1

    def ti(i):
        return nt - 1 - i if reverse else i

    vec = lambda: pl.BlockSpec((1, r), lambda bi, i: (0, 0))
    return pl.pallas_call(
        functools.partial(_lru_kernel, tm=tm, reverse=reverse, blocks=w_r.shape[0], halo=halo),
        grid=(b, nt),
        in_specs=[
            pl.BlockSpec((1, tm, r), lambda bi, i: (bi, ti(i), 1)),
            pl.BlockSpec((1, halo, r), lambda bi, i: (bi, jnp.maximum(ti(i) * nh - 1, 0), 1)),
            pl.BlockSpec((1, halo, r), lambda bi, i: (bi, jnp.minimum((ti(i) + 1) * nh, last), 1)),
            pl.BlockSpec(conv_w.shape, lambda bi, i: (0, 0)),
            vec(),
            pl.BlockSpec(w_r.shape, lambda bi, i: (0, 0, 0)),
            pl.BlockSpec(w_i.shape, lambda bi, i: (0, 0, 0)),
            vec(), vec(), vec(),
            pl.BlockSpec((1, 1, r), lambda bi, i: (bi, 0, 0)),
        ],
        out_specs=pl.BlockSpec((1, tm, r), lambda bi, i: (bi, ti(i), 0)),
        out_shape=jax.ShapeDtypeStruct((b, s, r), F32),
        scratch_shapes=[pltpu.VMEM((tm + 2 * halo, r), F32), pltpu.VMEM((tm, r), F32),
                        pltpu.VMEM((tm, r), F32), pltpu.VMEM((1, r), F32)],
        compiler_params=_cp("parallel", "arbitrary"),
        name="lru_scan",
    )(z, z, z, conv_w, conv_b.reshape(1, r), w_r, w_i, b_r.reshape(1, r), b_i.reshape(1, r),
      lam.reshape(1, r), h0.reshape(b, 1, r))


def _lru_out_prologue(pos, hf_ref, hb_ref, gate_ref, h_scr):
    g = gate_ref[0]
    gelu = 0.5 * g * (1.0 + jnp.tanh(0.7978845608028654 * (g + 0.044715 * (g * g * g))))
    h_scr[...] = ((hf_ref[0] + hb_ref[0]) * gelu).astype(BF16)


def _row_spec(tm, cols, col_blk=0):
    return pl.BlockSpec((1, tm, cols), lambda bi, i, j: (bi, i, col_blk))


def _tiles(s):
    return min(ROW_TILE, s)


def mlstm_mixer(xl, xc, g, mod_l, mod_c, w_in, w_gate, b_gate, head_g, w_out, need_ctx):
    b, s, d = xl.shape
    heads = ML_HEADS
    nv = w_out.shape[0]
    dv = nv // heads
    nqk = (w_in.shape[1] - 2 * nv) // 2
    dqk = nqk // heads
    n_in = w_in.shape[1]
    ngate = w_gate.shape[1]
    w = jnp.concatenate([w_in, w_gate, jnp.zeros((d, LANE - ngate), F32)], axis=1).astype(BF16)
    bias = jnp.concatenate([jnp.zeros((n_in,), F32), b_gate, jnp.zeros((LANE - ngate,), F32)])
    n_all = n_in + LANE
    tn = n_all // 7 if n_all % (7 * LANE) == 0 else LANE
    cos, sin = _rope_tables(s, dqk)
    w_out_b = w_out.astype(BF16)

    def project(x, mod):
        z = norm_proj(x, g, mod, w, tm=_tiles(x.shape[1]), tn=tn, bias=bias)
        gates_t = jnp.swapaxes(z[:, :, n_in:n_in + ngate], 1, 2)
        return z, gates_t

    def scan(z, gt, state, reverse, rotary):
        sx = z.shape[1]
        return mlstm_scan(z, gt, cos[:sx], sin[:sx], state, reverse=reverse, rotary=rotary,
                          heads=heads, dqk=dqk, dv=dv, chunk=min(ML_CHUNK, sx))

    def readout(hf, hb, z, x, mod):
        tm = _tiles(x.shape[1])
        o_blk = (2 * nqk + nv) // nv
        specs = [_row_spec(tm, nv), _row_spec(tm, nv), _row_spec(tm, nv, o_blk),
                 pl.BlockSpec((1, nv), lambda bi, i, j: (0, 0))]
        pro = functools.partial(_mlstm_readout_prologue, heads=heads, dv=dv)
        return resid_proj(pro, [hf, hb, z, head_g.reshape(1, nv)], specs, nv, w_out_b, x, mod,
                          gate_row=2, tm=tm, tn=min(512, d))

    zero = (jnp.zeros((b, heads, dqk, dv), F32), jnp.zeros((b, heads, 1, dqk), F32),
            jnp.zeros((b, heads, 1, LANE), F32))
    zc, gtc = project(xc, mod_c)
    zl, gtl = project(xl, mod_l)
    hc_f, st_f = scan(zc, gtc, zero, False, False)
    hc_b, st_b = scan(zc, gtc, zero, True, False)
    hl_f, _ = scan(zl, gtl, st_f, False, True)
    hl_b, _ = scan(zl, gtl, st_b, True, True)
    yl = readout(hl_f, hl_b, zl, xl, mod_l)
    yc = readout(hc_f, hc_b, zc, xc, mod_c) if need_ctx else None
    return yl, yc


def na_mixer(xl, xc, g, mod_l, mod_c, w_qkv, q_g, k_g, rpb, w_o, need_ctx):
    d = xl.shape[2]
    wb, wob = w_qkv.astype(BF16), w_o.astype(BF16)
    zl = norm_proj(xl, g, mod_l, wb, tm=_tiles(xl.shape[1]), tn=min(1024, wb.shape[1]))
    zc = norm_proj(xc, g, mod_c, wb, tm=_tiles(xc.shape[1]), tn=min(1024, wb.shape[1]))
    ol = na_attention(zl, zc, q_g, k_g, rpb, heads=NA_HEADS)

    def out(o, x, mod):
        tm = _tiles(x.shape[1])
        return resid_proj(_cast_prologue, [o], [_row_spec(tm, d)], d, wob, x, mod,
                          gate_row=2, tm=tm, tn=min(512, d))

    yl = out(ol, xl, mod_l)
    yc = out(ctx_attention(zc, q_g, k_g, heads=NA_HEADS), xc, mod_c) if need_ctx else None
    return yl, yc


def conformer_mixer(x, g, mod, w_pw1, dw, dw_b, ln_g, ln_b, w_pw2):
    b, s, d = x.shape
    width = dw.shape[0]
    halo = 16
    assert width // 2 <= halo
    tm = min(CONV_TILE, s)
    a = norm_proj(x, g, mod, w_pw1.astype(BF16), tm=_tiles(s), tn=min(512, d), glu=True)
    vec = lambda: pl.BlockSpec((1, d), lambda bi, i, j: (0, 0))
    specs = _halo_specs(tm, halo, s, d) + [pl.BlockSpec((width, d), lambda bi, i, j: (0, 0)),
                                           vec(), vec(), vec()]
    pro = functools.partial(_conformer_prologue, tm=tm, width=width, halo=halo)
    return resid_proj(pro, [a, a, a, dw, dw_b.reshape(1, d), ln_g.reshape(1, d), ln_b.reshape(1, d)],
                      specs, d, w_pw2.astype(BF16), x, mod, gate_row=2, tm=tm, tn=min(512, d),
                      scratch=[pltpu.VMEM((tm + 2 * halo, d), F32), pltpu.VMEM((tm, d), F32)])


def rglru_mixer(xl, xc, g, mod_l, mod_c, w_in, conv_w, conv_b, w_gate, b_gate, lam, w_out, need_ctx):
    b, s, d = xl.shape
    r = w_out.shape[0]
    wb, wob, wgb = w_in.astype(BF16), w_out.astype(BF16), w_gate.astype(BF16)
    zl = norm_proj(xl, g, mod_l, wb, tm=_tiles(s), tn=min(1024, wb.shape[1]))
    zc = norm_proj(xc, g, mod_c, wb, tm=_tiles(xc.shape[1]), tn=min(1024, wb.shape[1]))

    def scan(z, h0, reverse):
        k = 2 if reverse else 0
        return lru_scan(z, h0, conv_w, conv_b, wgb[k], wgb[k + 1], b_gate[k], b_gate[k + 1],
                        lam[1 if reverse else 0], reverse=reverse, tm=min(LRU_TILE, z.shape[1]))

    h0 = jnp.zeros((b, r), F32)
    hc_f = scan(zc, h0, False)
    hc_b = scan(zc, h0, True)
    hl_f = scan(zl, hc_f[:, -1], False)
    hl_b = scan(zl, hc_b[:, 0], True)

    def out(hf, hb, z, x, mod):
        tm = _tiles(x.shape[1])
        specs = [_row_spec(tm, r), _row_spec(tm, r), _row_spec(tm, r, 0)]
        return resid_proj(_lru_out_prologue, [hf, hb, z], specs, r, wob, x, mod,
                          gate_row=2, tm=tm, tn=min(512, d))

    yl = out(hl_f, hl_b, zl, xl, mod_l)
    yc = out(hc_f, hc_b, zc, xc, mod_c) if need_ctx else None
    return yl, yc


def kernel(x, c, ctx, c_ctx, norm_mix, norm_ffn, ada_w, ada_b, ml_w_in, ml_w_gate, ml_b_gate, ml_head_g, ml_w_out, na_w_qkv, na_q_g, na_k_g, na_rpb, na_w_o, cv_w_pw1, cv_dw, cv_dw_b, cv_ln_g, cv_ln_b, cv_w_pw2, lr_w_in, lr_conv, lr_conv_b, lr_w_gate, lr_b_gate, lr_lambda, lr_w_out, ffn_w_gu, ffn_conv, ffn_w_down):
    b, s, d = x.shape
    depth = ada_w.shape[0]
    n_mixers = 4
    cc_rows = -(-(b + 1) // SUBLANE) * SUBLANE
    cc = jnp.concatenate([c, c_ctx[None, :], jnp.zeros((cc_rows - b - 1, d), F32)], axis=0)
    mods = ada_modulation(cc, ada_w, ada_b).reshape(depth, cc_rows, 6, d)
    mods = jnp.pad(mods, ((0, 0), (0, 0), (0, MOD_ROWS - 6), (0, 0)))

    xl, xc = x, ctx
    for i in range(depth):
        kind, j = i % n_mixers, i // n_mixers
        need_ctx = i < depth - 1
        mod_l, mod_c = mods[i, :b], mods[i, b:b + 1]
        g = norm_mix[i]
        if kind == 0:
            xl, xc_new = mlstm_mixer(xl, xc, g, mod_l, mod_c, ml_w_in[j], ml_w_gate[j], ml_b_gate[j],
                                     ml_head_g[j], ml_w_out[j], need_ctx)
        elif kind == 1:
            xl, xc_new = na_mixer(xl, xc, g, mod_l, mod_c, na_w_qkv[j], na_q_g[j], na_k_g[j],
                                  na_rpb[j], na_w_o[j], need_ctx)
        elif kind == 2:
            cv = (cv_w_pw1[j], cv_dw[j], cv_dw_b[j], cv_ln_g[j], cv_ln_b[j], cv_w_pw2[j])
            xl, xc_new = conformer_mixer(xl, g, mod_l, *cv), None
            if need_ctx:
                xc_new = conformer_mixer(xc, g, mod_c, *cv)
        else:
            xl, xc_new = rglru_mixer(xl, xc, g, mod_l, mod_c, lr_w_in[j], lr_conv[j], lr_conv_b[j],
                                     lr_w_gate[j], lr_b_gate[j], lr_lambda[j], lr_w_out[j], need_ctx)
        wgu, wdn = ffn_w_gu[i].astype(BF16), ffn_w_down[i].astype(BF16)
        fc = 512 if wdn.shape[0] % 512 == 0 else LANE
        xl = conv_ffn(xl, norm_ffn[i], mod_l, wgu, ffn_conv[i], wdn, tm=_tiles(s), fc=fc)
        if need_ctx:
            xc = conv_ffn(xc_new, norm_ffn[i], mod_c, wgu, ffn_conv[i], wdn, tm=_tiles(xc.shape[1]), fc=fc)
    return xl
```

```python
import functools

import numpy as np
import jax
import jax.numpy as jnp
from jax import lax
from jax.experimental import pallas as pl
from jax.experimental.pallas import tpu as pltpu

F32 = jnp.float32
BF16 = jnp.bfloat16

EPS = 1e-6
ROPE_THETA = 10000.0
GRID_W = 64
ML_HEADS = 8
ML_CHUNK = 256
NA_HEADS = 16
WIN_ROWS = 8
WIN_COLS = 16
NA_QROWS = 4
NA_HEADS_PER_STEP = 2
LRU_BLOCKS = 8
LRU_C = 8.0
NEG = -1e30

V7X_VMEM_LIMIT = 56 * 1024 * 1024
LANE = 128
SUBLANE = 8
MOD_ROWS = 8
ROW_TILE = 512
CONV_TILE = 256
LRU_TILE = 256


def _cp(*sem):
    return pltpu.CompilerParams(dimension_semantics=sem, vmem_limit_bytes=V7X_VMEM_LIMIT)


def _dot(a, b):
    return jnp.dot(a, b, preferred_element_type=F32)


def _dot_nt(a, b):
    return lax.dot_general(a, b, (((1,), (1,)), ((), ())), preferred_element_type=F32)


def _sigmoid(x):
    return 1.0 / (1.0 + jnp.exp(-x))


def _log_sigmoid(x):
    return jnp.minimum(x, 0.0) - jnp.log(1.0 + jnp.exp(-jnp.abs(x)))


def _softplus(x):
    return jnp.maximum(x, 0.0) + jnp.log(1.0 + jnp.exp(-jnp.abs(x)))


def _rms(x, g):
    return x * lax.rsqrt(jnp.mean(x * x, axis=-1, keepdims=True) + EPS) * g


def _norm_mod(x, g, shift, scale):
    return _rms(x, g) * (1.0 + scale) + shift


def _pick(tile, n):
    assert n % tile == 0, (tile, n)
    return tile


def _ada_kernel(c_ref, w_ref, b_ref, o_ref):
    c = c_ref[...]
    s = (c * _sigmoid(c)).astype(BF16)
    o_ref[0] = _dot(s, w_ref[0].astype(BF16)) + b_ref[0]


def ada_modulation(cc, ada_w, ada_b):
    depth, d, n = ada_w.shape
    rows = cc.shape[0]
    tn = _pick(1024 if n % 1024 == 0 else n, n)
    return pl.pallas_call(
        _ada_kernel,
        grid=(depth, n // tn),
        in_specs=[pl.BlockSpec((rows, d), lambda l, j: (0, 0)),
                  pl.BlockSpec((1, d, tn), lambda l, j: (l, 0, j)),
                  pl.BlockSpec((1, 1, tn), lambda l, j: (l, 0, j))],
        out_specs=pl.BlockSpec((1, rows, tn), lambda l, j: (l, 0, j)),
        out_shape=jax.ShapeDtypeStruct((depth, rows, n), F32),
        compiler_params=_cp("parallel", "parallel"),
        name="ada_modulation",
    )(cc, ada_w, ada_b.reshape(depth, 1, n))


def _proj_kernel(x_ref, g_ref, mod_ref, *rest, glu, bias):
    if glu:
        wa_ref, wg_ref, o_ref, h_scr = rest
    elif bias:
        w_ref, b_ref, o_ref, h_scr = rest
    else:
        w_ref, o_ref, h_scr = rest

    @pl.when(pl.program_id(2) == 0)
    def _():
        h = _norm_mod(x_ref[0], g_ref[...], mod_ref[0, 0:1, :], mod_ref[0, 1:2, :])
        h_scr[...] = h.astype(BF16)

    h = h_scr[...]
    if glu:
        o_ref[0] = _dot(h, wa_ref[...]) * _sigmoid(_dot(h, wg_ref[...]))
    elif bias:
        o_ref[0] = _dot(h, w_ref[...]) + b_ref[...]
    else:
        o_ref[0] = _dot(h, w_ref[...])


def norm_proj(x, g, mod, w, *, tm, tn, bias=None, glu=False):
    b, s, d = x.shape
    n = w.shape[1] // 2 if glu else w.shape[1]
    tm, tn = _pick(tm, s), _pick(tn, n)
    per_batch = mod.shape[0] != 1
    in_specs = [pl.BlockSpec((1, tm, d), lambda bi, i, j: (bi, i, 0)),
                pl.BlockSpec((1, d), lambda bi, i, j: (0, 0)),
                pl.BlockSpec((1, MOD_ROWS, d), lambda bi, i, j: (bi if per_batch else 0, 0, 0))]
    args = [x, g.reshape(1, d), mod]
    if glu:
        in_specs += [pl.BlockSpec((d, tn), lambda bi, i, j: (0, j)),
                     pl.BlockSpec((d, tn), lambda bi, i, j: (0, j + n // tn))]
        args += [w, w]
    else:
        in_specs.append(pl.BlockSpec((d, tn), lambda bi, i, j: (0, j)))
        args.append(w)
        if bias is not None:
            in_specs.append(pl.BlockSpec((1, tn), lambda bi, i, j: (0, j)))
            args.append(bias.reshape(1, n))
    return pl.pallas_call(
        functools.partial(_proj_kernel, glu=glu, bias=bias is not None),
        grid=(b, s // tm, n // tn),
        in_specs=in_specs,
        out_specs=pl.BlockSpec((1, tm, tn), lambda bi, i, j: (bi, i, j)),
        out_shape=jax.ShapeDtypeStruct((b, s, n), F32),
        scratch_shapes=[pltpu.VMEM((tm, d), BF16)],
        compiler_params=_cp("parallel", "parallel", "arbitrary"),
        name="norm_proj",
    )(*args)


def _resid_kernel(*refs, prologue, n_in, gate_row):
    in_refs = refs[:n_in]
    w_ref, x_ref, mod_ref, o_ref, h_scr, *extra_scr = refs[n_in:]
    pos = (pl.program_id(1), pl.num_programs(1))

    @pl.when(pl.program_id(2) == 0)
    def _():
        prologue(pos, *in_refs, h_scr, *extra_scr)

    y = _dot(h_scr[...], w_ref[...])
    o_ref[0] = x_ref[0] + mod_ref[0, gate_row:gate_row + 1, :] * y


def resid_proj(prologue, ins, in_specs, k, w, x, mod, *, gate_row, tm, tn, scratch=()):
    b, s, d = x.shape
    n = w.shape[1]
    per_batch = mod.shape[0] != 1
    specs = list(in_specs) + [
        pl.BlockSpec((k, tn), lambda bi, i, j: (0, j)),
        pl.BlockSpec((1, tm, tn), lambda bi, i, j: (bi, i, j)),
        pl.BlockSpec((1, MOD_ROWS, tn), lambda bi, i, j: (bi if per_batch else 0, 0, j))]
    return pl.pallas_call(
        functools.partial(_resid_kernel, prologue=prologue, n_in=len(ins), gate_row=gate_row),
        grid=(b, s // tm, n // tn),
        in_specs=specs,
        out_specs=pl.BlockSpec((1, tm, tn), lambda bi, i, j: (bi, i, j)),
        out_shape=jax.ShapeDtypeStruct((b, s, n), F32),
        scratch_shapes=[pltpu.VMEM((tm, k), BF16)] + list(scratch),
        compiler_params=_cp("parallel", "parallel", "arbitrary"),
        name="resid_proj",
    )(*ins, w, x, mod)


def _ffn_kernel(x_ref, xp_ref, xn_ref, g_ref, mod_ref, wg_ref, wu_ref, cw_ref, wd_ref, o_ref,
                h_scr, hh_scr, g_scr, acc_scr, *, tm):
    i, j = pl.program_id(1), pl.program_id(2)
    nt, nf = pl.num_programs(1), pl.num_programs(2)
    hb = SUBLANE

    @pl.when(j == 0)
    def _():
        g, shift, scale = g_ref[...], mod_ref[0, 3:4, :], mod_ref[0, 4:5, :]
        h_scr[...] = _norm_mod(x_ref[0], g, shift, scale).astype(BF16)
        halo = jnp.concatenate([xp_ref[0], xn_ref[0]], axis=0)
        hh_scr[...] = _norm_mod(halo, g, shift, scale).astype(BF16)
        acc_scr[...] = jnp.zeros_like(acc_scr)

    gg = _dot(h_scr[...], wg_ref[...])
    gh = _dot(hh_scr[...], wg_ref[...])
    uu = _dot(h_scr[...], wu_ref[...])
    g_scr[0:hb] = jnp.where(i > 0, gh[0:hb], 0.0)
    g_scr[hb:hb + tm] = gg
    g_scr[hb + tm:] = jnp.where(i < nt - 1, gh[hb:], 0.0)
    cw = cw_ref[...]
    gc = (cw[0:1] * g_scr[pl.ds(hb - 1, tm), :] + cw[1:2] * gg
          + cw[2:3] * g_scr[pl.ds(hb + 1, tm), :])
    act = (gc * _sigmoid(gc) * uu).astype(BF16)
    acc_scr[...] += _dot(act, wd_ref[...])

    @pl.when(j == nf - 1)
    def _():
        o_ref[0] = x_ref[0] + mod_ref[0, 5:6, :] * acc_scr[...]


def conv_ffn(x, g, mod, w_gu, conv_w, w_down, *, tm, fc):
    b, s, d = x.shape
    f = w_down.shape[0]
    tm, fc = _pick(tm, s), _pick(fc, f)
    per_batch = mod.shape[0] != 1
    nh = tm // SUBLANE
    last_h = s // SUBLANE - 1
    return pl.pallas_call(
        functools.partial(_ffn_kernel, tm=tm),
        grid=(b, s // tm, f // fc),
        in_specs=[
            pl.BlockSpec((1, tm, d), lambda bi, i, j: (bi, i, 0)),
            pl.BlockSpec((1, SUBLANE, d), lambda bi, i, j: (bi, jnp.maximum(i * nh - 1, 0), 0)),
            pl.BlockSpec((1, SUBLANE, d), lambda bi, i, j: (bi, jnp.minimum((i + 1) * nh, last_h), 0)),
            pl.BlockSpec((1, d), lambda bi, i, j: (0, 0)),
            pl.BlockSpec((1, MOD_ROWS, d), lambda bi, i, j: (bi if per_batch else 0, 0, 0)),
            pl.BlockSpec((d, fc), lambda bi, i, j: (0, j)),
            pl.BlockSpec((d, fc), lambda bi, i, j: (0, j + f // fc)),
            pl.BlockSpec((conv_w.shape[0], fc), lambda bi, i, j: (0, j)),
            pl.BlockSpec((fc, d), lambda bi, i, j: (j, 0)),
        ],
        out_specs=pl.BlockSpec((1, tm, d), lambda bi, i, j: (bi, i, 0)),
        out_shape=jax.ShapeDtypeStruct((b, s, d), F32),
        scratch_shapes=[pltpu.VMEM((tm, d), BF16),
                        pltpu.VMEM((2 * SUBLANE, d), BF16),
                        pltpu.VMEM((tm + 2 * SUBLANE, fc), F32),
                        pltpu.VMEM((tm, d), F32)],
        compiler_params=_cp("parallel", "parallel", "arbitrary"),
        name="conv_ffn",
    )(x, x, x, g.reshape(1, d), mod, w_gu, w_gu, conv_w, w_down)


def _split3(x):
    hi = x.astype(BF16)
    r = x - hi.astype(F32)
    mid = r.astype(BF16)
    lo = (r - mid.astype(F32)).astype(BF16)
    return hi, mid, lo


def _rope(x, cos, sin_signed):
    half = x.shape[-1] // 4
    lane = lax.broadcasted_iota(jnp.int32, x.shape, 1)
    partner = jnp.where((lane % (2 * half)) < half,
                        pltpu.roll(x, x.shape[-1] - half, 1), pltpu.roll(x, half, 1))
    return x * cos + partner * sin_signed


def _mlstm_kernel(q_ref, k_ref, v_ref, gc_ref, gr_ref, cos_ref, sin_ref, c0_ref, n0_ref, m0_ref,
                  h_ref, c_out, n_out, m_out, c_scr, n_scr, m_scr, *, reverse, rotary, heads, dqk, dv):
    c = pl.program_id(1)
    nc = pl.num_programs(1)
    L = q_ref.shape[1]

    @pl.when(c == 0)
    def _():
        c_scr[...] = c0_ref[0]
        n_scr[...] = n0_ref[0]
        m_scr[...] = m0_ref[0]

    off_i = 2 * heads if reverse else 0
    off_f = off_i + heads
    gcol = gc_ref[0]
    grow = gr_ref[0]
    lane = lax.broadcasted_iota(jnp.int32, gcol.shape, 1)

    row = lax.broadcasted_iota(jnp.int32, (L, L), 0)
    col = lax.broadcasted_iota(jnp.int32, (L, L), 1)
    mask = (col >= row) if reverse else (col <= row)
    mask_t = (row >= col) if reverse else (row <= col)
    tri = jnp.where(mask, 1.0, 0.0).astype(BF16)
    tri_t = jnp.where(mask_t, 1.0, 0.0).astype(BF16)
    lf_rows = _log_sigmoid(grow)
    b_all = sum(_dot(tri, p) for p in _split3(_log_sigmoid(gcol)))
    b_rows = sum(_dot(p, tri_t) for p in _split3(lf_rows))

    for hd in range(heads):
        qs, vs = slice(hd * dqk, (hd + 1) * dqk), slice(hd * dv, (hd + 1) * dv)
        q = q_ref[0, :, qs] * (dqk ** -0.5)
        k = k_ref[0, :, qs]
        if rotary:
            q = _rope(q, cos_ref[...], sin_ref[...])
            k = _rope(k, cos_ref[...], sin_ref[...])
        qb, kb, vb = q.astype(BF16), k.astype(BF16), v_ref[0, :, vs].astype(BF16)

        li_col = jnp.sum(jnp.where(lane == off_i + hd, gcol, 0.0), axis=-1, keepdims=True)
        b_col = jnp.sum(jnp.where(lane == off_f + hd, b_all, 0.0), axis=-1, keepdims=True)
        li_row = grow[off_i + hd:off_i + hd + 1]
        b_row = b_rows[off_f + hd:off_f + hd + 1]
        total = jnp.sum(lf_rows[off_f + hd:off_f + hd + 1], axis=-1, keepdims=True)

        m_prev = m_scr[hd, 0:1, 0:1]
        dmat = jnp.where(mask, b_col - b_row + li_row, NEG)
        inter = b_col + m_prev
        m_t = jnp.maximum(inter, jnp.max(dmat, axis=-1, keepdims=True))
        w_inter = jnp.exp(inter - m_t)
        s_qk = _dot_nt(qb, kb) * jnp.exp(dmat - m_t)
        num = w_inter * _dot(qb, c_scr[hd].astype(BF16)) + _dot(s_qk.astype(BF16), vb)
        den = (w_inter * jnp.sum(q * n_scr[hd], axis=-1, keepdims=True)
               + jnp.sum(s_qk, axis=-1, keepdims=True))
        h_ref[0, :, vs] = num / jnp.maximum(jnp.abs(den), jnp.exp(-m_t))

        w_col = total - b_col + li_col
        m_new = jnp.maximum(total + m_prev, jnp.max(w_col, axis=0, keepdims=True))
        decay = jnp.exp(total + m_prev - m_new)
        ek = k * jnp.exp(w_col - m_new)
        c_scr[hd] = decay * c_scr[hd] + _dot(ek.T.astype(BF16), vb)
        n_scr[hd] = decay * n_scr[hd] + jnp.sum(ek, axis=0, keepdims=True)
        m_scr[hd] = jnp.broadcast_to(m_new, (1, m_scr.shape[2]))

    @pl.when(c == nc - 1)
    def _():
        c_out[0] = c_scr[...]
        n_out[0] = n_scr[...]
        m_out[0] = m_scr[...]


def mlstm_scan(z, gates_t, cos, sin, state, *, reverse, rotary, heads, dqk, dv, chunk):
    b, s, _ = z.shape
    L = _pick(chunk, s)
    nc = s // L
    nqk, nv = heads * dqk, heads * dv
    assert (2 * nqk) % nv == 0
    gate_blk = (2 * nqk + 2 * nv) // LANE

    def ci(c):
        return nc - 1 - c if reverse else c

    c0, n0, m0 = state
    st_specs = [pl.BlockSpec((1, heads, dqk, dv), lambda bi, c: (bi, 0, 0, 0)),
                pl.BlockSpec((1, heads, 1, dqk), lambda bi, c: (bi, 0, 0, 0)),
                pl.BlockSpec((1, heads, 1, LANE), lambda bi, c: (bi, 0, 0, 0))]
    h, c1, n1, m1 = pl.pallas_call(
        functools.partial(_mlstm_kernel, reverse=reverse, rotary=rotary, heads=heads, dqk=dqk, dv=dv),
        grid=(b, nc),
        in_specs=[
            pl.BlockSpec((1, L, nqk), lambda bi, c: (bi, ci(c), 0)),
            pl.BlockSpec((1, L, nqk), lambda bi, c: (bi, ci(c), 1)),
            pl.BlockSpec((1, L, nv), lambda bi, c: (bi, ci(c), 2 * nqk // nv)),
            pl.BlockSpec((1, L, LANE), lambda bi, c: (bi, ci(c), gate_blk)),
            pl.BlockSpec((1, 4 * heads, L), lambda bi, c: (bi, 0, ci(c))),
            pl.BlockSpec((L, dqk), lambda bi, c: (ci(c), 0)),
            pl.BlockSpec((L, dqk), lambda bi, c: (ci(c), 0)),
        ] + st_specs,
        out_specs=[pl.BlockSpec((1, L, nv), lambda bi, c: (bi, ci(c), 0))] + st_specs,
        out_shape=[jax.ShapeDtypeStruct((b, s, nv), F32),
                   jax.ShapeDtypeStruct(c0.shape, F32),
                   jax.ShapeDtypeStruct(n0.shape, F32),
                   jax.ShapeDtypeStruct(m0.shape, F32)],
        scratch_shapes=[pltpu.VMEM((heads, dqk, dv), F32), pltpu.VMEM((heads, 1, dqk), F32),
                        pltpu.VMEM((heads, 1, LANE), F32)],
        compiler_params=_cp("parallel", "arbitrary"),
        name="mlstm_scan",
    )(z, z, z, z, gates_t, cos, sin, c0, n0, m0)
    return h, (c1, n1, m1)


def _rope_tables(s, d):
    half, quarter = d // 2, d // 4
    freqs = ROPE_THETA ** (-jnp.arange(0, half, 2, dtype=F32) / half)
    t = jnp.arange(s)
    ang_r = (t // GRID_W).astype(F32)[:, None] * freqs[None, :]
    ang_c = (t % GRID_W).astype(F32)[:, None] * freqs[None, :]
    cos = jnp.concatenate([jnp.cos(ang_r)] * 2 + [jnp.cos(ang_c)] * 2, axis=-1)
    sin = jnp.concatenate([-jnp.sin(ang_r), jnp.sin(ang_r), -jnp.sin(ang_c), jnp.sin(ang_c)], axis=-1)
    del quarter
    return cos, sin


def _mlstm_readout_prologue(pos, hf_ref, hb_ref, o_ref, g_ref, h_scr, *, heads, dv):
    for hd in range(heads):
        sl = slice(hd * dv, (hd + 1) * dv)
        hs = hf_ref[0, :, sl] + hb_ref[0, :, sl]
        y = _rms(hs, g_ref[:, sl]) * _sigmoid(o_ref[0, :, sl])
        h_scr[:, sl] = y.astype(BF16)


def _na_kernel(q_ref, k_ref, v_ref, kc_ref, vc_ref, qg_ref, kg_ref, bias_ref, o_ref,
               kn_scr, vn_scr, kcn_scr, vcn_scr, *, nkeys, kstep, half, kr_max, hps, dh):
    rb = pl.program_id(2)
    nb = pl.num_programs(2)
    tq = q_ref.shape[1]

    @pl.when(rb == 0)
    def _():
        for hh in range(hps):
            sl = slice(hh * dh, (hh + 1) * dh)
            kn_scr[hh] = _rms(k_ref[0, :, sl], kg_ref[...]).astype(BF16)
            kcn_scr[hh] = _rms(kc_ref[0, :, sl], kg_ref[...]).astype(BF16)
            vn_scr[hh, :, 0:dh] = v_ref[0, :, sl].astype(BF16)
            vn_scr[hh, :, dh:] = jnp.ones((vn_scr.shape[1], dh), BF16)
            vcn_scr[hh, :, 0:dh] = vc_ref[0, :, sl].astype(BF16)
            vcn_scr[hh, :, dh:] = jnp.ones((vcn_scr.shape[1], dh), BF16)

    kr0 = jnp.clip(rb * (tq // kstep) - half, 0, kr_max)
    start = pl.multiple_of(kr0 * kstep, kstep)
    cls = jnp.where(rb == 0, 0, jnp.where(rb == nb - 1, 2, 1))
    for hh in range(hps):
        sl = slice(hh * dh, (hh + 1) * dh)
        qn = (_rms(q_ref[0, :, sl], qg_ref[...]) * (dh ** -0.5)).astype(BF16)
        s_loc = _dot_nt(qn, kn_scr[hh, pl.ds(start, nkeys), :]) + bias_ref[hh, cls]
        s_ctx = _dot_nt(qn, kcn_scr[hh])
        m = jnp.maximum(jnp.max(s_loc, axis=-1, keepdims=True), jnp.max(s_ctx, axis=-1, keepdims=True))
        p_loc = jnp.exp(s_loc - m).astype(BF16)
        p_ctx = jnp.exp(s_ctx - m).astype(BF16)
        ol = _dot(p_loc, vn_scr[hh, pl.ds(start, nkeys), :]) + _dot(p_ctx, vcn_scr[hh])
        o_ref[0, :, sl] = ol[:, 0:dh] / ol[:, dh:dh + 1]


def _na_bias_tables(rows):
    W, R, half = GRID_W, NA_QROWS, WIN_ROWS // 2
    win_r = min(WIN_ROWS, rows)
    U = R + win_r - 1
    nb = rows // R
    kr_max = rows - U
    assert nb >= 3 and R >= half and kr_max >= 0
    ri, kj = np.arange(R)[:, None], np.arange(U)[None, :]
    per_block = []
    for rb in range(nb):
        r = rb * R + ri
        kr = np.clip(rb * R - half, 0, kr_max) + kj
        rs = np.clip(r - half, 0, rows - win_r)
        ok = (kr >= rs) & (kr < rs + win_r)
        dr = np.clip(kr - r + WIN_ROWS - 1, 0, 2 * WIN_ROWS - 2)
        per_block.append((ok, dr))
    for rb in range(2, nb - 1):
        assert all(np.array_equal(a, b_) for a, b_ in zip(per_block[1], per_block[rb]))
    return [per_block[0], per_block[1], per_block[nb - 1]], U, kr_max


def _na_bias_kernel(rpb_ref, o_ref, t_scr, *, classes):
    h = pl.program_id(0)
    W = t_scr.shape[1]
    ndr, ndc = 2 * WIN_ROWS - 1, 2 * WIN_COLS - 1
    c = lax.broadcasted_iota(jnp.int32, (W, W), 0)
    kc = lax.broadcasted_iota(jnp.int32, (W, W), 1)
    cstart = jnp.clip(c - WIN_COLS // 2, 0, W - WIN_COLS)
    dcidx = jnp.clip(kc - c + WIN_COLS - 1, 0, ndc - 1)
    for dr in range(ndr):
        t = jnp.full((W, W), NEG, F32)
        for dc in range(ndc):
            t = jnp.where(dcidx == dc, rpb_ref[h, dr * ndc + dc], t)
        t = jnp.where(kc >= cstart, t, NEG)
        t_scr[dr] = jnp.where(kc < cstart + WIN_COLS, t, NEG)
    o_ref[...] = jnp.full(o_ref.shape, NEG, F32)
    for cls, (ok, dr) in enumerate(classes):
        for ri in range(ok.shape[0]):
            for kj in range(ok.shape[1]):
                if ok[ri, kj]:
                    o_ref[0, cls, ri * W:(ri + 1) * W, kj * W:(kj + 1) * W] = t_scr[int(dr[ri, kj])]


def na_bias_table(rpb, classes, tq, nkeys):
    heads = rpb.shape[0]
    return pl.pallas_call(
        functools.partial(_na_bias_kernel, classes=classes),
        grid=(heads,),
        in_specs=[pl.BlockSpec(memory_space=pltpu.SMEM)],
        out_specs=pl.BlockSpec((1, len(classes), tq, nkeys), lambda h: (h, 0, 0, 0)),
        out_shape=jax.ShapeDtypeStruct((heads, len(classes), tq, nkeys), F32),
        scratch_shapes=[pltpu.VMEM((2 * WIN_ROWS - 1, GRID_W, GRID_W), F32)],
        compiler_params=_cp("parallel"),
        name="na_bias_table",
    )(rpb.reshape(heads, -1))


def na_attention(z, zc, q_g, k_g, rpb, *, heads):
    b, s, n3 = z.shape
    dh = n3 // (3 * heads)
    cl = zc.shape[1]
    rows = s // GRID_W
    classes, U, kr_max = _na_bias_tables(rows)
    tq, nkeys = NA_QROWS * GRID_W, U * GRID_W
    bias = na_bias_table(rpb, classes, tq, nkeys)
    hps = NA_HEADS_PER_STEP
    hg, w = heads // hps, hps * dh
    return pl.pallas_call(
        functools.partial(_na_kernel, nkeys=nkeys, kstep=GRID_W, half=WIN_ROWS // 2, kr_max=kr_max,
                          hps=hps, dh=dh),
        grid=(b, hg, s // tq),
        in_specs=[
            pl.BlockSpec((1, tq, w), lambda bi, h, r: (bi, r, h)),
            pl.BlockSpec((1, s, w), lambda bi, h, r: (bi, 0, hg + h)),
            pl.BlockSpec((1, s, w), lambda bi, h, r: (bi, 0, 2 * hg + h)),
            pl.BlockSpec((1, cl, w), lambda bi, h, r: (bi, 0, hg + h)),
            pl.BlockSpec((1, cl, w), lambda bi, h, r: (bi, 0, 2 * hg + h)),
            pl.BlockSpec((1, dh), lambda bi, h, r: (0, 0)),
            pl.BlockSpec((1, dh), lambda bi, h, r: (0, 0)),
            pl.BlockSpec((hps, 3, tq, nkeys), lambda bi, h, r: (h, 0, 0, 0)),
        ],
        out_specs=pl.BlockSpec((1, tq, w), lambda bi, h, r: (bi, r, h)),
        out_shape=jax.ShapeDtypeStruct((b, s, heads * dh), F32),
        scratch_shapes=[pltpu.VMEM((hps, s, dh), BF16), pltpu.VMEM((hps, s, 2 * dh), BF16),
                        pltpu.VMEM((hps, cl, dh), BF16), pltpu.VMEM((hps, cl, 2 * dh), BF16)],
        compiler_params=_cp("parallel", "parallel", "arbitrary"),
        name="na_attention",
    )(z, z, z, zc, zc, q_g.reshape(1, dh), k_g.reshape(1, dh), bias)


def _ctx_attn_kernel(q_ref, k_ref, v_ref, qg_ref, kg_ref, o_ref):
    dh = q_ref.shape[2]
    qn = (_rms(q_ref[0], qg_ref[...]) * (dh ** -0.5)).astype(BF16)
    kn = _rms(k_ref[0], kg_ref[...]).astype(BF16)
    s = _dot_nt(qn, kn)
    p = jnp.exp(s - jnp.max(s, axis=-1, keepdims=True))
    l = jnp.sum(p, axis=-1, keepdims=True)
    o_ref[0] = _dot(p.astype(BF16), v_ref[0].astype(BF16)) / l


def ctx_attention(zc, q_g, k_g, *, heads):
    b, cl, n3 = zc.shape
    dh = n3 // (3 * heads)
    return pl.pallas_call(
        _ctx_attn_kernel,
        grid=(b, heads),
        in_specs=[pl.BlockSpec((1, cl, dh), lambda bi, h: (bi, 0, h)),
                  pl.BlockSpec((1, cl, dh), lambda bi, h: (bi, 0, heads + h)),
                  pl.BlockSpec((1, cl, dh), lambda bi, h: (bi, 0, 2 * heads + h)),
                  pl.BlockSpec((1, dh), lambda bi, h: (0, 0)),
                  pl.BlockSpec((1, dh), lambda bi, h: (0, 0))],
        out_specs=pl.BlockSpec((1, cl, dh), lambda bi, h: (bi, 0, h)),
        out_shape=jax.ShapeDtypeStruct((b, cl, heads * dh), F32),
        compiler_params=_cp("parallel", "parallel"),
        name="ctx_attention",
    )(zc, zc, zc, q_g.reshape(1, dh), k_g.reshape(1, dh))


def _cast_prologue(pos, a_ref, h_scr):
    h_scr[...] = a_ref[0].astype(BF16)


def _conv_rows(tm, width):
    rsub = min(tm, 128)
    return rsub, rsub + (-(-width // SUBLANE) - 1) * SUBLANE


def _conformer_prologue(pos, a_ref, ap_ref, an_ref, dw_ref, dwb_ref, lg_ref, lb_ref, h_scr, ext_scr, y_scr,
                        sh_scr, *, tm, width, halo):
    i, nt = pos
    d = a_ref.shape[2]
    pad = width // 2
    ext_scr[0:halo] = jnp.where(i > 0, ap_ref[0], 0.0)
    ext_scr[halo:halo + tm] = a_ref[0]
    ext_scr[halo + tm:] = jnp.where(i < nt - 1, an_ref[0], 0.0)

    rsub, span = _conv_rows(tm, width)
    n_a = -(-width // SUBLANE)

    def col_block(cb, carry):
        cs = pl.ds(pl.multiple_of(cb * LANE, LANE), LANE)
        for r0 in range(0, tm, rsub):
            acc = jnp.broadcast_to(dwb_ref[:, cs], (rsub, LANE))
            for b in range(SUBLANE):
                sh_scr[b % 2] = ext_scr[pl.ds(halo - pad + r0 + b, span), cs]
                for a in range(n_a):
                    k = a * SUBLANE + b
                    if k < width:
                        acc = acc + dw_ref[k:k + 1, cs] * sh_scr[b % 2, a * SUBLANE:a * SUBLANE + rsub, :]
            y_scr[r0:r0 + rsub, cs] = acc
        return carry

    lax.fori_loop(0, d // LANE, col_block, 0)
    y = y_scr[...]
    mu = jnp.mean(y, axis=-1, keepdims=True)
    var = jnp.mean(jnp.square(y - mu), axis=-1, keepdims=True)
    yn = (y - mu) * lax.rsqrt(var + EPS) * lg_ref[...] + lb_ref[...]
    h_scr[...] = (yn * _sigmoid(yn)).astype(BF16)


def _halo_specs(tm, halo, s, width_cols, col_blk=0):
    nh = tm // halo
    last = s // halo - 1
    return [pl.BlockSpec((1, tm, width_cols), lambda bi, i, j: (bi, i, col_blk)),
            pl.BlockSpec((1, halo, width_cols), lambda bi, i, j: (bi, jnp.maximum(i * nh - 1, 0), col_blk)),
            pl.BlockSpec((1, halo, width_cols), lambda bi, i, j: (bi, jnp.minimum((i + 1) * nh, last), col_blk))]


def _lru_kernel(u_ref, up_ref, un_ref, cw_ref, cb_ref, wr_ref, wi_ref, br_ref, bi_ref, lam_ref, h0_ref,
                o_ref, ext_scr, a_scr, b_scr, carry_scr, *, tm, reverse, blocks, halo):
    i, nt = pl.program_id(1), pl.num_programs(1)
    d = u_ref.shape[2]
    bw = d // blocks
    first = (i == 0)
    ti = (nt - 1 - i) if reverse else i

    @pl.when(first)
    def _():
        carry_scr[...] = h0_ref[0]

    ext_scr[0:halo] = jnp.where(ti > 0, up_ref[0], 0.0)
    ext_scr[halo:halo + tm] = u_ref[0]
    ext_scr[halo + tm:] = jnp.where(ti < nt - 1, un_ref[0], 0.0)
    width = cw_ref.shape[0]
    pad_l = width // 2
    u = jnp.broadcast_to(cb_ref[...], (tm, d))
    for k in range(width):
        u = u + cw_ref[k:k + 1, :] * ext_scr[pl.ds(halo - pad_l + k, tm), :]

    ub = u.astype(BF16)
    neg_c_sp = -LRU_C * _softplus(-lam_ref[...])
    for blk in range(blocks):
        sl = slice(blk * bw, (blk + 1) * bw)
        r = _sigmoid(_dot(ub[:, sl], wr_ref[blk]) + br_ref[:, sl])
        g = _sigmoid(_dot(ub[:, sl], wi_ref[blk]) + bi_ref[:, sl])
        log_a = neg_c_sp[:, sl] * r
        a = jnp.exp(log_a)
        a_scr[:, sl] = a
        b_scr[:, sl] = jnp.sqrt(1.0 - a * a) * (g * u[:, sl])

    sub = lax.broadcasted_iota(jnp.int32, (SUBLANE, d), 0)
    ngroups = tm // SUBLANE

    def group(gi, carry):
        r0 = pl.multiple_of(((ngroups - 1 - gi) if reverse else gi) * SUBLANE, SUBLANE)
        a8 = a_scr[pl.ds(r0, SUBLANE), :]
        b8 = b_scr[pl.ds(r0, SUBLANE), :]
        for sft in (1, 2, 4):
            if reverse:
                valid = sub < SUBLANE - sft
                a_sh = pltpu.roll(a8, SUBLANE - sft, 0)
                b_sh = pltpu.roll(b8, SUBLANE - sft, 0)
            else:
                valid = sub >= sft
                a_sh = pltpu.roll(a8, sft, 0)
                b_sh = pltpu.roll(b8, sft, 0)
            b8 = jnp.where(valid, a8 * b_sh + b8, b8)
            a8 = jnp.where(valid, a8 * a_sh, a8)
        h8 = a8 * carry + b8
        o_ref[0, pl.ds(r0, SUBLANE), :] = h8
        return h8[0:1, :] if reverse else h8[SUBLANE - 1:SUBLANE, :]

    carry_scr[...] = lax.fori_loop(0, ngroups, group, carry_scr[...])


def lru_scan(z, h0, conv_w, conv_b, w_r, w_i, b_r, b_i, lam, *, reverse, tm):
    b, s, r2 = z.shape
    r = r2 // 2
    tm = _pick(tm, s)
    nt = s // tm
    halo = SUBLANE
    nh = tm // halo
    last = s // halo - 1

    def ti(i):
        return nt - 1 - i if reverse else i

    vec = lambda: pl.BlockSpec((1, r), lambda bi, i: (0, 0))
    return pl.pallas_call(
        functools.partial(_lru_kernel, tm=tm, reverse=reverse, blocks=w_r.shape[0], halo=halo),
        grid=(b, nt),
        in_specs=[
            pl.BlockSpec((1, tm, r), lambda bi, i: (bi, ti(i), 1)),
            pl.BlockSpec((1, halo, r), lambda bi, i: (bi, jnp.maximum(ti(i) * nh - 1, 0), 1)),
            pl.BlockSpec((1, halo, r), lambda bi, i: (bi, jnp.minimum((ti(i) + 1) * nh, last), 1)),
            pl.BlockSpec(conv_w.shape, lambda bi, i: (0, 0)),
            vec(),
            pl.BlockSpec(w_r.shape, lambda bi, i: (0, 0, 0)),
            pl.BlockSpec(w_i.shape, lambda bi, i: (0, 0, 0)),
            vec(), vec(), vec(),
            pl.BlockSpec((1, 1, r), lambda bi, i: (bi, 0, 0)),
        ],
        out_specs=pl.BlockSpec((1, tm, r), lambda bi, i: (bi, ti(i), 0)),
        out_shape=jax.ShapeDtypeStruct((b, s, r), F32),
        scratch_shapes=[pltpu.VMEM((tm + 2 * halo, r), F32), pltpu.VMEM((tm, r), F32),
                        pltpu.VMEM((tm, r), F32), pltpu.VMEM((1, r), F32)],
        compiler_params=_cp("parallel", "arbitrary"),
        name="lru_scan",
    )(z, z, z, conv_w, conv_b.reshape(1, r), w_r, w_i, b_r.reshape(1, r), b_i.reshape(1, r),
      lam.reshape(1, r), h0.reshape(b, 1, r))


def _lru_out_prologue(pos, hf_ref, hb_ref, gate_ref, h_scr):
    g = gate_ref[0]
    gelu = 0.5 * g * (1.0 + jnp.tanh(0.7978845608028654 * (g + 0.044715 * (g * g * g))))
    h_scr[...] = ((hf_ref[0] + hb_ref[0]) * gelu).astype(BF16)


def _row_spec(tm, cols, col_blk=0):
    return pl.BlockSpec((1, tm, cols), lambda bi, i, j: (bi, i, col_blk))


def _tiles(s):
    return min(ROW_TILE, s)


def mlstm_mixer(xl, xc, g, mod_l, mod_c, w_in, w_gate, b_gate, head_g, w_out, need_ctx):
    b, s, d = xl.shape
    heads = ML_HEADS
    nv = w_out.shape[0]
    dv = nv // heads
    nqk = (w_in.shape[1] - 2 * nv) // 2
    dqk = nqk // heads
    n_in = w_in.shape[1]
    ngate = w_gate.shape[1]
    w = jnp.concatenate([w_in, w_gate, jnp.zeros((d, LANE - ngate), F32)], axis=1).astype(BF16)
    bias = jnp.concatenate([jnp.zeros((n_in,), F32), b_gate, jnp.zeros((LANE - ngate,), F32)])
    n_all = n_in + LANE
    tn = n_all // 7 if n_all % (7 * LANE) == 0 else LANE
    cos, sin = _rope_tables(s, dqk)
    w_out_b = w_out.astype(BF16)

    def project(x, mod):
        z = norm_proj(x, g, mod, w, tm=_tiles(x.shape[1]), tn=tn, bias=bias)
        gates_t = jnp.swapaxes(z[:, :, n_in:n_in + ngate], 1, 2)
        return z, gates_t

    def scan(z, gt, state, reverse, rotary):
        sx = z.shape[1]
        return mlstm_scan(z, gt, cos[:sx], sin[:sx], state, reverse=reverse, rotary=rotary,
                          heads=heads, dqk=dqk, dv=dv, chunk=min(ML_CHUNK, sx))

    def readout(hf, hb, z, x, mod):
        tm = _tiles(x.shape[1])
        o_blk = (2 * nqk + nv) // nv
        specs = [_row_spec(tm, nv), _row_spec(tm, nv), _row_spec(tm, nv, o_blk),
                 pl.BlockSpec((1, nv), lambda bi, i, j: (0, 0))]
        pro = functools.partial(_mlstm_readout_prologue, heads=heads, dv=dv)
        return resid_proj(pro, [hf, hb, z, head_g.reshape(1, nv)], specs, nv, w_out_b, x, mod,
                          gate_row=2, tm=tm, tn=min(512, d))

    zero = (jnp.zeros((b, heads, dqk, dv), F32), jnp.zeros((b, heads, 1, dqk), F32),
            jnp.zeros((b, heads, 1, LANE), F32))
    zc, gtc = project(xc, mod_c)
    zl, gtl = project(xl, mod_l)
    hc_f, st_f = scan(zc, gtc, zero, False, False)
    hc_b, st_b = scan(zc, gtc, zero, True, False)
    hl_f, _ = scan(zl, gtl, st_f, False, True)
    hl_b, _ = scan(zl, gtl, st_b, True, True)
    yl = readout(hl_f, hl_b, zl, xl, mod_l)
    yc = readout(hc_f, hc_b, zc, xc, mod_c) if need_ctx else None
    return yl, yc


def na_mixer(xl, xc, g, mod_l, mod_c, w_qkv, q_g, k_g, rpb, w_o, need_ctx):
    d = xl.shape[2]
    wb, wob = w_qkv.astype(BF16), w_o.astype(BF16)
    zl = norm_proj(xl, g, mod_l, wb, tm=_tiles(xl.shape[1]), tn=min(1024, wb.shape[1]))
    zc = norm_proj(xc, g, mod_c, wb, tm=_tiles(xc.shape[1]), tn=min(1024, wb.shape[1]))
    ol = na_attention(zl, zc, q_g, k_g, rpb, heads=NA_HEADS)

    def out(o, x, mod):
        tm = _tiles(x.shape[1])
        return resid_proj(_cast_prologue, [o], [_row_spec(tm, d)], d, wob, x, mod,
                          gate_row=2, tm=tm, tn=min(512, d))

    yl = out(ol, xl, mod_l)
    yc = out(ctx_attention(zc, q_g, k_g, heads=NA_HEADS), xc, mod_c) if need_ctx else None
    return yl, yc


def conformer_mixer(x, g, mod, w_pw1, dw, dw_b, ln_g, ln_b, w_pw2):
    b, s, d = x.shape
    width = dw.shape[0]
    halo = 16
    assert width // 2 <= halo
    tm = min(CONV_TILE, s)
    a = norm_proj(x, g, mod, w_pw1.astype(BF16), tm=_tiles(s), tn=min(512, d), glu=True)
    vec = lambda: pl.BlockSpec((1, d), lambda bi, i, j: (0, 0))
    specs = _halo_specs(tm, halo, s, d) + [pl.BlockSpec((width, d), lambda bi, i, j: (0, 0)),
                                           vec(), vec(), vec()]
    pro = functools.partial(_conformer_prologue, tm=tm, width=width, halo=halo)
    return resid_proj(pro, [a, a, a, dw, dw_b.reshape(1, d), ln_g.reshape(1, d), ln_b.reshape(1, d)],
                      specs, d, w_pw2.astype(BF16), x, mod, gate_row=2, tm=tm, tn=min(512, d),
                      scratch=[pltpu.VMEM((tm + 2 * halo, d), F32), pltpu.VMEM((tm, d), F32),
                               pltpu.VMEM((2, _conv_rows(tm, width)[1], LANE), F32)])


def rglru_mixer(xl, xc, g, mod_l, mod_c, w_in, conv_w, conv_b, w_gate, b_gate, lam, w_out, need_ctx):
    b, s, d = xl.shape
    r = w_out.shape[0]
    wb, wob, wgb = w_in.astype(BF16), w_out.astype(BF16), w_gate.astype(BF16)
    zl = norm_proj(xl, g, mod_l, wb, tm=_tiles(s), tn=min(1024, wb.shape[1]))
    zc = norm_proj(xc, g, mod_c, wb, tm=_tiles(xc.shape[1]), tn=min(1024, wb.shape[1]))

    def scan(z, h0, reverse):
        k = 2 if reverse else 0
        return lru_scan(z, h0, conv_w, conv_b, wgb[k], wgb[k + 1], b_gate[k], b_gate[k + 1],
                        lam[1 if reverse else 0], reverse=reverse, tm=min(LRU_TILE, z.shape[1]))

    h0 = jnp.zeros((b, r), F32)
    hc_f = scan(zc, h0, False)
    hc_b = scan(zc, h0, True)
    hl_f = scan(zl, hc_f[:, -1], False)
    hl_b = scan(zl, hc_b[:, 0], True)

    def out(hf, hb, z, x, mod):
        tm = _tiles(x.shape[1])
        specs = [_row_spec(tm, r), _row_spec(tm, r), _row_spec(tm, r, 0)]
        return resid_proj(_lru_out_prologue, [hf, hb, z], specs, r, wob, x, mod,
                          gate_row=2, tm=tm, tn=min(512, d))

    yl = out(hl_f, hl_b, zl, xl, mod_l)
    yc = out(hc_f, hc_b, zc, xc, mod_c) if need_ctx else None
    return yl, yc


def kernel(x, c, ctx, c_ctx, norm_mix, norm_ffn, ada_w, ada_b, ml_w_in, ml_w_gate, ml_b_gate, ml_head_g, ml_w_out, na_w_qkv, na_q_g, na_k_g, na_rpb, na_w_o, cv_w_pw1, cv_dw, cv_dw_b, cv_ln_g, cv_ln_b, cv_w_pw2, lr_w_in, lr_conv, lr_conv_b, lr_w_gate, lr_b_gate, lr_lambda, lr_w_out, ffn_w_gu, ffn_conv, ffn_w_down):
    b, s, d = x.shape
    depth = ada_w.shape[0]
    n_mixers = 4
    cc_rows = -(-(b + 1) // SUBLANE) * SUBLANE
    cc = jnp.concatenate([c, c_ctx[None, :], jnp.zeros((cc_rows - b - 1, d), F32)], axis=0)
    mods = ada_modulation(cc, ada_w, ada_b).reshape(depth, cc_rows, 6, d)
    mods = jnp.pad(mods, ((0, 0), (0, 0), (0, MOD_ROWS - 6), (0, 0)))

    xl, xc = x, ctx
    for i in range(depth):
        kind, j = i % n_mixers, i // n_mixers
        need_ctx = i < depth - 1
        mod_l, mod_c = mods[i, :b], mods[i, b:b + 1]
        g = norm_mix[i]
        if kind == 0:
            xl, xc_new = mlstm_mixer(xl, xc, g, mod_l, mod_c, ml_w_in[j], ml_w_gate[j], ml_b_gate[j],
                                     ml_head_g[j], ml_w_out[j], need_ctx)
        elif kind == 1:
            xl, xc_new = na_mixer(xl, xc, g, mod_l, mod_c, na_w_qkv[j], na_q_g[j], na_k_g[j],
                                  na_rpb[j], na_w_o[j], need_ctx)
        elif kind == 2:
            cv = (cv_w_pw1[j], cv_dw[j], cv_dw_b[j], cv_ln_g[j], cv_ln_b[j], cv_w_pw2[j])
            xl, xc_new = conformer_mixer(xl, g, mod_l, *cv), None
            if need_ctx:
                xc_new = conformer_mixer(xc, g, mod_c, *cv)
        else:
            xl, xc_new = rglru_mixer(xl, xc, g, mod_l, mod_c, lr_w_in[j], lr_conv[j], lr_conv_b[j],
                                     lr_w_gate[j], lr_b_gate[j], lr_lambda[j], lr_w_out[j], need_ctx)
        wgu, wdn = ffn_w_gu[i].astype(BF16), ffn_w_down[i].astype(BF16)
        fc = 512 if wdn.shape[0] % 512 == 0 else LANE
        xl = conv_ffn(xl, norm_ffn[i], mod_l, wgu, ffn_conv[i], wdn, tm=_tiles(s), fc=fc)
        if need_ctx:
            xc = conv_ffn(xc_new, norm_ffn[i], mod_c, wgu, ffn_conv[i], wdn, tm=_tiles(xc.shape[1]), fc=fc)
    return xl
```

```python
import functools

import numpy as np
import jax
import jax.numpy as jnp
from jax import lax
from jax.experimental import pallas as pl
from jax.experimental.pallas import tpu as pltpu

F32 = jnp.float32
BF16 = jnp.bfloat16

EPS = 1e-6
ROPE_THETA = 10000.0
GRID_W = 64
ML_HEADS = 8
ML_CHUNK = 256
NA_HEADS = 16
WIN_ROWS = 8
WIN_COLS = 16
NA_QROWS = 4
NA_HEADS_PER_STEP = 2
LRU_BLOCKS = 8
LRU_C = 8.0
NEG = -1e30

V7X_VMEM_LIMIT = 56 * 1024 * 1024
LANE = 128
SUBLANE = 8
MOD_ROWS = 8
ROW_TILE = 512
PROJ_TILE = 256
PROJ_SUB = 128
MXU_WIDTH = 256
LRU_TILE = 256


def _cp(*sem):
    return pltpu.CompilerParams(dimension_semantics=sem, vmem_limit_bytes=V7X_VMEM_LIMIT)


def _dot(a, b):
    return jnp.dot(a, b, preferred_element_type=F32)


def _dot_nt(a, b):
    return lax.dot_general(a, b, (((1,), (1,)), ((), ())), preferred_element_type=F32)


def _sigmoid(x):
    return 1.0 / (1.0 + jnp.exp(-x))


def _log_sigmoid(x):
    return jnp.minimum(x, 0.0) - jnp.log(1.0 + jnp.exp(-jnp.abs(x)))


def _softplus(x):
    return jnp.maximum(x, 0.0) + jnp.log(1.0 + jnp.exp(-jnp.abs(x)))


def _rms(x, g):
    return x * lax.rsqrt(jnp.mean(x * x, axis=-1, keepdims=True) + EPS) * g


def _norm_mod(x, g, shift, scale):
    return _rms(x, g) * (1.0 + scale) + shift


def _pick(tile, n):
    assert n % tile == 0, (tile, n)
    return tile


def _ada_kernel(c_ref, w_ref, b_ref, o_ref):
    c = c_ref[...]
    s = (c * _sigmoid(c)).astype(BF16)
    o_ref[0] = _dot(s, w_ref[0].astype(BF16)) + b_ref[0]


def ada_modulation(cc, ada_w, ada_b):
    depth, d, n = ada_w.shape
    rows = cc.shape[0]
    tn = _pick(1024 if n % 1024 == 0 else n, n)
    return pl.pallas_call(
        _ada_kernel,
        grid=(depth, n // tn),
        in_specs=[pl.BlockSpec((rows, d), lambda l, j: (0, 0)),
                  pl.BlockSpec((1, d, tn), lambda l, j: (l, 0, j)),
                  pl.BlockSpec((1, 1, tn), lambda l, j: (l, 0, j))],
        out_specs=pl.BlockSpec((1, rows, tn), lambda l, j: (l, 0, j)),
        out_shape=jax.ShapeDtypeStruct((depth, rows, n), F32),
        compiler_params=_cp("parallel", "parallel"),
        name="ada_modulation",
    )(cc, ada_w, ada_b.reshape(depth, 1, n))


def _col_chunk(n):
    for c in (1536, 1280, 1024, 768, 512, 256, 128):
        if n % c == 0:
            return c
    return n


def _resident(shape):
    return pl.BlockSpec(shape, lambda *_: (0,) * len(shape), pipeline_mode=pl.Buffered(1))


def _proj_kernel(x_ref, g_ref, mod_ref, w_ref, *rest, glu, bias, sub):
    if bias:
        b_ref, o_ref = rest
    else:
        (o_ref,) = rest
    tm, n = o_ref.shape[1], o_ref.shape[2]
    nchunk = _col_chunk(n)
    shift = mod_ref[0, 0:1, :]
    gain = g_ref[...] * (1.0 + mod_ref[0, 1:2, :])
    for r0 in range(0, tm, sub):
        x = x_ref[0, r0:r0 + sub, :]
        h = (x * lax.rsqrt(jnp.mean(x * x, axis=-1, keepdims=True) + EPS) * gain + shift).astype(BF16)
        for c0 in range(0, n, nchunk):
            y = _dot(h, w_ref[:, c0:c0 + nchunk])
            if glu:
                y = y * _sigmoid(_dot(h, w_ref[:, n + c0:n + c0 + nchunk]))
            if bias:
                y = y + b_ref[:, c0:c0 + nchunk]
            o_ref[0, r0:r0 + sub, c0:c0 + nchunk] = y


def norm_proj(x, g, mod, w, *, bias=None, glu=False):
    b, s, d = x.shape
    n = w.shape[1] // 2 if glu else w.shape[1]
    tm = min(PROJ_TILE, s)
    sub = min(PROJ_SUB, tm)
    assert s % tm == 0 and tm % sub == 0
    per_batch = mod.shape[0] != 1
    in_specs = [pl.BlockSpec((1, tm, d), lambda bi, i: (bi, i, 0)),
                pl.BlockSpec((1, d), lambda bi, i: (0, 0)),
                pl.BlockSpec((1, MOD_ROWS, d), lambda bi, i: (bi if per_batch else 0, 0, 0)),
                _resident(w.shape)]
    args = [x, g.reshape(1, d), mod, w]
    if bias is not None:
        in_specs.append(pl.BlockSpec((1, n), lambda bi, i: (0, 0)))
        args.append(bias.reshape(1, n))
    return pl.pallas_call(
        functools.partial(_proj_kernel, glu=glu, bias=bias is not None, sub=sub),
        grid=(b, s // tm),
        in_specs=in_specs,
        out_specs=pl.BlockSpec((1, tm, n), lambda bi, i: (bi, i, 0)),
        out_shape=jax.ShapeDtypeStruct((b, s, n), F32),
        compiler_params=_cp("parallel", "parallel"),
        name="norm_proj",
    )(*args)


def _resid_kernel(*refs, prologue, n_in, gate_row, sub):
    in_refs = refs[:n_in]
    w_ref, x_ref, mod_ref, o_ref, *scr = refs[n_in:]
    pos = (pl.program_id(1), pl.num_programs(1))
    tm, n = x_ref.shape[1], x_ref.shape[2]
    nchunk = _col_chunk(n)
    gate = mod_ref[0, gate_row:gate_row + 1, :]
    for r0 in range(0, tm, sub):
        h = prologue(pos, r0, sub, *in_refs, *scr)
        for c0 in range(0, n, nchunk):
            y = _dot(h, w_ref[:, c0:c0 + nchunk])
            o_ref[0, r0:r0 + sub, c0:c0 + nchunk] = (x_ref[0, r0:r0 + sub, c0:c0 + nchunk]
                                                    + gate[:, c0:c0 + nchunk] * y)


def resid_proj(prologue, ins, in_specs, w, x, mod, *, gate_row, tm, scratch=()):
    b, s, n = x.shape
    sub = min(PROJ_SUB, tm)
    assert s % tm == 0 and tm % sub == 0
    per_batch = mod.shape[0] != 1
    specs = list(in_specs) + [
        _resident(w.shape),
        pl.BlockSpec((1, tm, n), lambda bi, i: (bi, i, 0)),
        pl.BlockSpec((1, MOD_ROWS, n), lambda bi, i: (bi if per_batch else 0, 0, 0))]
    return pl.pallas_call(
        functools.partial(_resid_kernel, prologue=prologue, n_in=len(ins), gate_row=gate_row, sub=sub),
        grid=(b, s // tm),
        in_specs=specs,
        out_specs=pl.BlockSpec((1, tm, n), lambda bi, i: (bi, i, 0)),
        out_shape=jax.ShapeDtypeStruct((b, s, n), F32),
        scratch_shapes=list(scratch),
        compiler_params=_cp("parallel", "parallel"),
        name="resid_proj",
    )(*ins, w, x, mod)


def _ffn_kernel(x_ref, xp_ref, xn_ref, g_ref, mod_ref, wg_ref, wu_ref, cw_ref, wd_ref, o_ref,
                h_scr, hh_scr, g_scr, acc_scr, *, tm):
    i, j = pl.program_id(1), pl.program_id(2)
    nt, nf = pl.num_programs(1), pl.num_programs(2)
    hb = SUBLANE

    @pl.when(j == 0)
    def _():
        g, shift, scale = g_ref[...], mod_ref[0, 3:4, :], mod_ref[0, 4:5, :]
        h_scr[...] = _norm_mod(x_ref[0], g, shift, scale).astype(BF16)
        halo = jnp.concatenate([xp_ref[0], xn_ref[0]], axis=0)
        hh_scr[...] = _norm_mod(halo, g, shift, scale).astype(BF16)
        acc_scr[...] = jnp.zeros_like(acc_scr)

    gg = _dot(h_scr[...], wg_ref[...])
    gh = _dot(hh_scr[...], wg_ref[...])
    uu = _dot(h_scr[...], wu_ref[...])
    g_scr[0:hb] = jnp.where(i > 0, gh[0:hb], 0.0)
    g_scr[hb:hb + tm] = gg
    g_scr[hb + tm:] = jnp.where(i < nt - 1, gh[hb:], 0.0)
    cw = cw_ref[...]
    gc = (cw[0:1] * g_scr[pl.ds(hb - 1, tm), :] + cw[1:2] * gg
          + cw[2:3] * g_scr[pl.ds(hb + 1, tm), :])
    act = (gc * _sigmoid(gc) * uu).astype(BF16)
    acc_scr[...] += _dot(act, wd_ref[...])

    @pl.when(j == nf - 1)
    def _():
        o_ref[0] = x_ref[0] + mod_ref[0, 5:6, :] * acc_scr[...]


def conv_ffn(x, g, mod, w_gu, conv_w, w_down, *, tm, fc):
    b, s, d = x.shape
    f = w_down.shape[0]
    tm, fc = _pick(tm, s), _pick(fc, f)
    per_batch = mod.shape[0] != 1
    nh = tm // SUBLANE
    last_h = s // SUBLANE - 1
    return pl.pallas_call(
        functools.partial(_ffn_kernel, tm=tm),
        grid=(b, s // tm, f // fc),
        in_specs=[
            pl.BlockSpec((1, tm, d), lambda bi, i, j: (bi, i, 0)),
            pl.BlockSpec((1, SUBLANE, d), lambda bi, i, j: (bi, jnp.maximum(i * nh - 1, 0), 0)),
            pl.BlockSpec((1, SUBLANE, d), lambda bi, i, j: (bi, jnp.minimum((i + 1) * nh, last_h), 0)),
            pl.BlockSpec((1, d), lambda bi, i, j: (0, 0)),
            pl.BlockSpec((1, MOD_ROWS, d), lambda bi, i, j: (bi if per_batch else 0, 0, 0)),
            pl.BlockSpec((d, fc), lambda bi, i, j: (0, j)),
            pl.BlockSpec((d, fc), lambda bi, i, j: (0, j + f // fc)),
            pl.BlockSpec((conv_w.shape[0], fc), lambda bi, i, j: (0, j)),
            pl.BlockSpec((fc, d), lambda bi, i, j: (j, 0)),
        ],
        out_specs=pl.BlockSpec((1, tm, d), lambda bi, i, j: (bi, i, 0)),
        out_shape=jax.ShapeDtypeStruct((b, s, d), F32),
        scratch_shapes=[pltpu.VMEM((tm, d), BF16),
                        pltpu.VMEM((2 * SUBLANE, d), BF16),
                        pltpu.VMEM((tm + 2 * SUBLANE, fc), F32),
                        pltpu.VMEM((tm, d), F32)],
        compiler_params=_cp("parallel", "parallel", "arbitrary"),
        name="conv_ffn",
    )(x, x, x, g.reshape(1, d), mod, w_gu, w_gu, conv_w, w_down)


def _split3(x):
    hi = x.astype(BF16)
    r = x - hi.astype(F32)
    mid = r.astype(BF16)
    lo = (r - mid.astype(F32)).astype(BF16)
    return hi, mid, lo


def _rope(x, cos, sin_signed):
    half = x.shape[-1] // 4
    lane = lax.broadcasted_iota(jnp.int32, x.shape, 1)
    partner = jnp.where((lane % (2 * half)) < half,
                        pltpu.roll(x, x.shape[-1] - half, 1), pltpu.roll(x, half, 1))
    return x * cos + partner * sin_signed


def _mlstm_kernel(q_ref, k_ref, v_ref, gc_ref, gr_ref, cos_ref, sin_ref, c0_ref, n0_ref, m0_ref,
                  h_ref, c_out, n_out, m_out, c_scr, n_scr, m_scr, *, reverse, rotary, heads, dqk, dv):
    c = pl.program_id(1)
    nc = pl.num_programs(1)
    L = q_ref.shape[1]

    @pl.when(c == 0)
    def _():
        c_scr[...] = c0_ref[0]
        n_scr[...] = n0_ref[0]
        m_scr[...] = m0_ref[0]

    off_i = 2 * heads if reverse else 0
    off_f = off_i + heads
    gcol = gc_ref[0]
    grow = gr_ref[0]
    lane = lax.broadcasted_iota(jnp.int32, gcol.shape, 1)

    row = lax.broadcasted_iota(jnp.int32, (L, L), 0)
    col = lax.broadcasted_iota(jnp.int32, (L, L), 1)
    mask = (col >= row) if reverse else (col <= row)
    mask_t = (row >= col) if reverse else (row <= col)
    tri = jnp.where(mask, 1.0, 0.0).astype(BF16)
    tri_t = jnp.where(mask_t, 1.0, 0.0).astype(BF16)
    lf_rows = _log_sigmoid(grow)
    b_all = sum(_dot(tri, p) for p in _split3(_log_sigmoid(gcol)))
    b_rows = sum(_dot(p, tri_t) for p in _split3(lf_rows))

    for hd in range(heads):
        qs, vs = slice(hd * dqk, (hd + 1) * dqk), slice(hd * dv, (hd + 1) * dv)
        q = q_ref[0, :, qs] * (dqk ** -0.5)
        k = k_ref[0, :, qs]
        if rotary:
            q = _rope(q, cos_ref[...], sin_ref[...])
            k = _rope(k, cos_ref[...], sin_ref[...])
        qb, kb, vb = q.astype(BF16), k.astype(BF16), v_ref[0, :, vs].astype(BF16)

        li_col = jnp.sum(jnp.where(lane == off_i + hd, gcol, 0.0), axis=-1, keepdims=True)
        b_col = jnp.sum(jnp.where(lane == off_f + hd, b_all, 0.0), axis=-1, keepdims=True)
        li_row = grow[off_i + hd:off_i + hd + 1]
        b_row = b_rows[off_f + hd:off_f + hd + 1]
        total = jnp.sum(lf_rows[off_f + hd:off_f + hd + 1], axis=-1, keepdims=True)

        m_prev = m_scr[hd, 0:1, 0:1]
        dmat = jnp.where(mask, b_col - b_row + li_row, NEG)
        inter = b_col + m_prev
        m_t = jnp.maximum(inter, jnp.max(dmat, axis=-1, keepdims=True))
        w_inter = jnp.exp(inter - m_t)
        s_qk = _dot_nt(qb, kb) * jnp.exp(dmat - m_t)
        num = w_inter * _dot(qb, c_scr[hd].astype(BF16)) + _dot(s_qk.astype(BF16), vb)
        den = (w_inter * jnp.sum(q * n_scr[hd], axis=-1, keepdims=True)
               + jnp.sum(s_qk, axis=-1, keepdims=True))
        h_ref[0, :, vs] = num / jnp.maximum(jnp.abs(den), jnp.exp(-m_t))

        w_col = total - b_col + li_col
        m_new = jnp.maximum(total + m_prev, jnp.max(w_col, axis=0, keepdims=True))
        decay = jnp.exp(total + m_prev - m_new)
        ek = k * jnp.exp(w_col - m_new)
        c_scr[hd] = decay * c_scr[hd] + _dot(ek.T.astype(BF16), vb)
        n_scr[hd] = decay * n_scr[hd] + jnp.sum(ek, axis=0, keepdims=True)
        m_scr[hd] = jnp.broadcast_to(m_new, (1, m_scr.shape[2]))

    @pl.when(c == nc - 1)
    def _():
        c_out[0] = c_scr[...]
        n_out[0] = n_scr[...]
        m_out[0] = m_scr[...]


def mlstm_scan(z, gates_t, cos, sin, state, *, reverse, rotary, heads, dqk, dv, chunk):
    b, s, _ = z.shape
    L = _pick(chunk, s)
    nc = s // L
    nqk, nv = heads * dqk, heads * dv
    assert (2 * nqk) % nv == 0
    gate_blk = (2 * nqk + 2 * nv) // LANE

    def ci(c):
        return nc - 1 - c if reverse else c

    c0, n0, m0 = state
    st_specs = [pl.BlockSpec((1, heads, dqk, dv), lambda bi, c: (bi, 0, 0, 0)),
                pl.BlockSpec((1, heads, 1, dqk), lambda bi, c: (bi, 0, 0, 0)),
                pl.BlockSpec((1, heads, 1, LANE), lambda bi, c: (bi, 0, 0, 0))]
    h, c1, n1, m1 = pl.pallas_call(
        functools.partial(_mlstm_kernel, reverse=reverse, rotary=rotary, heads=heads, dqk=dqk, dv=dv),
        grid=(b, nc),
        in_specs=[
            pl.BlockSpec((1, L, nqk), lambda bi, c: (bi, ci(c), 0)),
            pl.BlockSpec((1, L, nqk), lambda bi, c: (bi, ci(c), 1)),
            pl.BlockSpec((1, L, nv), lambda bi, c: (bi, ci(c), 2 * nqk // nv)),
            pl.BlockSpec((1, L, LANE), lambda bi, c: (bi, ci(c), gate_blk)),
            pl.BlockSpec((1, 4 * heads, L), lambda bi, c: (bi, 0, ci(c))),
            pl.BlockSpec((L, dqk), lambda bi, c: (ci(c), 0)),
            pl.BlockSpec((L, dqk), lambda bi, c: (ci(c), 0)),
        ] + st_specs,
        out_specs=[pl.BlockSpec((1, L, nv), lambda bi, c: (bi, ci(c), 0))] + st_specs,
        out_shape=[jax.ShapeDtypeStruct((b, s, nv), F32),
                   jax.ShapeDtypeStruct(c0.shape, F32),
                   jax.ShapeDtypeStruct(n0.shape, F32),
                   jax.ShapeDtypeStruct(m0.shape, F32)],
        scratch_shapes=[pltpu.VMEM((heads, dqk, dv), F32), pltpu.VMEM((heads, 1, dqk), F32),
                        pltpu.VMEM((heads, 1, LANE), F32)],
        compiler_params=_cp("parallel", "arbitrary"),
        name="mlstm_scan",
    )(z, z, z, z, gates_t, cos, sin, c0, n0, m0)
    return h, (c1, n1, m1)


def _rope_tables(s, d):
    half, quarter = d // 2, d // 4
    freqs = ROPE_THETA ** (-jnp.arange(0, half, 2, dtype=F32) / half)
    t = jnp.arange(s)
    ang_r = (t // GRID_W).astype(F32)[:, None] * freqs[None, :]
    ang_c = (t % GRID_W).astype(F32)[:, None] * freqs[None, :]
    cos = jnp.concatenate([jnp.cos(ang_r)] * 2 + [jnp.cos(ang_c)] * 2, axis=-1)
    sin = jnp.concatenate([-jnp.sin(ang_r), jnp.sin(ang_r), -jnp.sin(ang_c), jnp.sin(ang_c)], axis=-1)
    del quarter
    return cos, sin


def _mlstm_readout_prologue(pos, r0, rows, hf_ref, hb_ref, o_ref, g_ref, *, heads, dv):
    rs = slice(r0, r0 + rows)
    parts = []
    for hd in range(heads):
        sl = slice(hd * dv, (hd + 1) * dv)
        hs = hf_ref[0, rs, sl] + hb_ref[0, rs, sl]
        parts.append((_rms(hs, g_ref[:, sl]) * _sigmoid(o_ref[0, rs, sl])).astype(BF16))
    return jnp.concatenate(parts, axis=-1)


def _na_kernel(q_ref, k_ref, v_ref, kc_ref, vc_ref, qg_ref, kg_ref, bias_ref, o_ref,
               kn_scr, vn_scr, kcn_scr, vcn_scr, *, nkeys, kstep, half, kr_max, hps, dh):
    rb = pl.program_id(2)
    nb = pl.num_programs(2)
    tq = q_ref.shape[1]

    @pl.when(rb == 0)
    def _():
        for hh in range(hps):
            sl = slice(hh * dh, (hh + 1) * dh)
            kn_scr[hh] = _rms(k_ref[0, :, sl], kg_ref[...]).astype(BF16)
            kcn_scr[hh] = _rms(kc_ref[0, :, sl], kg_ref[...]).astype(BF16)
            vn_scr[hh, :, 0:dh] = v_ref[0, :, sl].astype(BF16)
            vn_scr[hh, :, dh:] = jnp.ones((vn_scr.shape[1], dh), BF16)
            vcn_scr[hh, :, 0:dh] = vc_ref[0, :, sl].astype(BF16)
            vcn_scr[hh, :, dh:] = jnp.ones((vcn_scr.shape[1], dh), BF16)

    kr0 = jnp.clip(rb * (tq // kstep) - half, 0, kr_max)
    start = pl.multiple_of(kr0 * kstep, kstep)
    cls = jnp.where(rb == 0, 0, jnp.where(rb == nb - 1, 2, 1))
    for hh in range(hps):
        sl = slice(hh * dh, (hh + 1) * dh)
        qn = (_rms(q_ref[0, :, sl], qg_ref[...]) * (dh ** -0.5)).astype(BF16)
        s_loc = _dot_nt(qn, kn_scr[hh, pl.ds(start, nkeys), :]) + bias_ref[hh, cls]
        s_ctx = _dot_nt(qn, kcn_scr[hh])
        m = jnp.maximum(jnp.max(s_loc, axis=-1, keepdims=True), jnp.max(s_ctx, axis=-1, keepdims=True))
        p_loc = jnp.exp(s_loc - m).astype(BF16)
        p_ctx = jnp.exp(s_ctx - m).astype(BF16)
        ol = _dot(p_loc, vn_scr[hh, pl.ds(start, nkeys), :]) + _dot(p_ctx, vcn_scr[hh])
        o_ref[0, :, sl] = ol[:, 0:dh] / ol[:, dh:dh + 1]


def _na_bias_tables(rows):
    W, R, half = GRID_W, NA_QROWS, WIN_ROWS // 2
    win_r = min(WIN_ROWS, rows)
    U = R + win_r - 1
    nb = rows // R
    kr_max = rows - U
    assert nb >= 3 and R >= half and kr_max >= 0
    ri, kj = np.arange(R)[:, None], np.arange(U)[None, :]
    per_block = []
    for rb in range(nb):
        r = rb * R + ri
        kr = np.clip(rb * R - half, 0, kr_max) + kj
        rs = np.clip(r - half, 0, rows - win_r)
        ok = (kr >= rs) & (kr < rs + win_r)
        dr = np.clip(kr - r + WIN_ROWS - 1, 0, 2 * WIN_ROWS - 2)
        per_block.append((ok, dr))
    for rb in range(2, nb - 1):
        assert all(np.array_equal(a, b_) for a, b_ in zip(per_block[1], per_block[rb]))
    return [per_block[0], per_block[1], per_block[nb - 1]], U, kr_max


def _na_bias_kernel(rpb_ref, o_ref, t_scr, *, classes):
    h = pl.program_id(0)
    W = t_scr.shape[1]
    ndr, ndc = 2 * WIN_ROWS - 1, 2 * WIN_COLS - 1
    c = lax.broadcasted_iota(jnp.int32, (W, W), 0)
    kc = lax.broadcasted_iota(jnp.int32, (W, W), 1)
    cstart = jnp.clip(c - WIN_COLS // 2, 0, W - WIN_COLS)
    dcidx = jnp.clip(kc - c + WIN_COLS - 1, 0, ndc - 1)
    for dr in range(ndr):
        t = jnp.full((W, W), NEG, F32)
        for dc in range(ndc):
            t = jnp.where(dcidx == dc, rpb_ref[h, dr * ndc + dc], t)
        t = jnp.where(kc >= cstart, t, NEG)
        t_scr[dr] = jnp.where(kc < cstart + WIN_COLS, t, NEG)
    o_ref[...] = jnp.full(o_ref.shape, NEG, F32)
    for cls, (ok, dr) in enumerate(classes):
        for ri in range(ok.shape[0]):
            for kj in range(ok.shape[1]):
                if ok[ri, kj]:
                    o_ref[0, cls, ri * W:(ri + 1) * W, kj * W:(kj + 1) * W] = t_scr[int(dr[ri, kj])]


def na_bias_table(rpb, classes, tq, nkeys):
    heads = rpb.shape[0]
    return pl.pallas_call(
        functools.partial(_na_bias_kernel, classes=classes),
        grid=(heads,),
        in_specs=[pl.BlockSpec(memory_space=pltpu.SMEM)],
        out_specs=pl.BlockSpec((1, len(classes), tq, nkeys), lambda h: (h, 0, 0, 0)),
        out_shape=jax.ShapeDtypeStruct((heads, len(classes), tq, nkeys), F32),
        scratch_shapes=[pltpu.VMEM((2 * WIN_ROWS - 1, GRID_W, GRID_W), F32)],
        compiler_params=_cp("parallel"),
        name="na_bias_table",
    )(rpb.reshape(heads, -1))


def na_attention(z, zc, q_g, k_g, rpb, *, heads):
    b, s, n3 = z.shape
    dh = n3 // (3 * heads)
    cl = zc.shape[1]
    rows = s // GRID_W
    classes, U, kr_max = _na_bias_tables(rows)
    tq, nkeys = NA_QROWS * GRID_W, U * GRID_W
    bias = na_bias_table(rpb, classes, tq, nkeys)
    hps = NA_HEADS_PER_STEP
    hg, w = heads // hps, hps * dh
    return pl.pallas_call(
        functools.partial(_na_kernel, nkeys=nkeys, kstep=GRID_W, half=WIN_ROWS // 2, kr_max=kr_max,
                          hps=hps, dh=dh),
        grid=(b, hg, s // tq),
        in_specs=[
            pl.BlockSpec((1, tq, w), lambda bi, h, r: (bi, r, h)),
            pl.BlockSpec((1, s, w), lambda bi, h, r: (bi, 0, hg + h)),
            pl.BlockSpec((1, s, w), lambda bi, h, r: (bi, 0, 2 * hg + h)),
            pl.BlockSpec((1, cl, w), lambda bi, h, r: (bi, 0, hg + h)),
            pl.BlockSpec((1, cl, w), lambda bi, h, r: (bi, 0, 2 * hg + h)),
            pl.BlockSpec((1, dh), lambda bi, h, r: (0, 0)),
            pl.BlockSpec((1, dh), lambda bi, h, r: (0, 0)),
            pl.BlockSpec((hps, 3, tq, nkeys), lambda bi, h, r: (h, 0, 0, 0)),
        ],
        out_specs=pl.BlockSpec((1, tq, w), lambda bi, h, r: (bi, r, h)),
        out_shape=jax.ShapeDtypeStruct((b, s, heads * dh), F32),
        scratch_shapes=[pltpu.VMEM((hps, s, dh), BF16), pltpu.VMEM((hps, s, 2 * dh), BF16),
                        pltpu.VMEM((hps, cl, dh), BF16), pltpu.VMEM((hps, cl, 2 * dh), BF16)],
        compiler_params=_cp("parallel", "parallel", "arbitrary"),
        name="na_attention",
    )(z, z, z, zc, zc, q_g.reshape(1, dh), k_g.reshape(1, dh), bias)


def _ctx_attn_kernel(q_ref, k_ref, v_ref, qg_ref, kg_ref, o_ref):
    dh = q_ref.shape[2]
    qn = (_rms(q_ref[0], qg_ref[...]) * (dh ** -0.5)).astype(BF16)
    kn = _rms(k_ref[0], kg_ref[...]).astype(BF16)
    s = _dot_nt(qn, kn)
    p = jnp.exp(s - jnp.max(s, axis=-1, keepdims=True))
    l = jnp.sum(p, axis=-1, keepdims=True)
    o_ref[0] = _dot(p.astype(BF16), v_ref[0].astype(BF16)) / l


def ctx_attention(zc, q_g, k_g, *, heads):
    b, cl, n3 = zc.shape
    dh = n3 // (3 * heads)
    return pl.pallas_call(
        _ctx_attn_kernel,
        grid=(b, heads),
        in_specs=[pl.BlockSpec((1, cl, dh), lambda bi, h: (bi, 0, h)),
                  pl.BlockSpec((1, cl, dh), lambda bi, h: (bi, 0, heads + h)),
                  pl.BlockSpec((1, cl, dh), lambda bi, h: (bi, 0, 2 * heads + h)),
                  pl.BlockSpec((1, dh), lambda bi, h: (0, 0)),
                  pl.BlockSpec((1, dh), lambda bi, h: (0, 0))],
        out_specs=pl.BlockSpec((1, cl, dh), lambda bi, h: (bi, 0, h)),
        out_shape=jax.ShapeDtypeStruct((b, cl, heads * dh), F32),
        compiler_params=_cp("parallel", "parallel"),
        name="ctx_attention",
    )(zc, zc, zc, q_g.reshape(1, dh), k_g.reshape(1, dh))


def _cast_prologue(pos, r0, rows, a_ref):
    return a_ref[0, r0:r0 + rows, :].astype(BF16)


def _conv_rows(tm, width):
    rsub = min(tm, PROJ_SUB)
    return rsub, rsub + (-(-width // SUBLANE) - 1) * SUBLANE


def _conformer_prologue(pos, r0, rows, a_ref, ap_ref, an_ref, dw_ref, dwb_ref, lg_ref, lb_ref, ext_scr, y_scr,
                        sh_scr, *, tm, width, halo):
    i, nt = pos
    d = a_ref.shape[2]
    pad = width // 2
    if r0 == 0:
        ext_scr[0:halo] = jnp.where(i > 0, ap_ref[0], 0.0)
        ext_scr[halo:halo + tm] = a_ref[0]
        ext_scr[halo + tm:] = jnp.where(i < nt - 1, an_ref[0], 0.0)

    rsub, span = _conv_rows(tm, width)
    assert rows == rsub
    n_a = -(-width // SUBLANE)

    def col_block(cb, carry):
        cs = pl.ds(pl.multiple_of(cb * LANE, LANE), LANE)
        acc = jnp.broadcast_to(dwb_ref[:, cs], (rsub, LANE))
        for b in range(SUBLANE):
            slot = b % sh_scr.shape[0]
            sh_scr[slot] = ext_scr[pl.ds(halo - pad + r0 + b, span), cs]
            for a in range(n_a):
                k = a * SUBLANE + b
                if k < width:
                    acc = acc + dw_ref[k:k + 1, cs] * sh_scr[slot, a * SUBLANE:a * SUBLANE + rsub, :]
        y_scr[r0:r0 + rsub, cs] = acc
        return carry

    lax.fori_loop(0, d // LANE, col_block, 0)
    y = y_scr[r0:r0 + rsub, :]
    mu = jnp.mean(y, axis=-1, keepdims=True)
    var = jnp.mean(jnp.square(y - mu), axis=-1, keepdims=True)
    yn = (y - mu) * lax.rsqrt(var + EPS) * lg_ref[...] + lb_ref[...]
    return (yn * _sigmoid(yn)).astype(BF16)


def _halo_specs(tm, halo, s, width_cols, col_blk=0):
    nh = tm // halo
    last = s // halo - 1
    return [pl.BlockSpec((1, tm, width_cols), lambda bi, i: (bi, i, col_blk)),
            pl.BlockSpec((1, halo, width_cols), lambda bi, i: (bi, jnp.maximum(i * nh - 1, 0), col_blk)),
            pl.BlockSpec((1, halo, width_cols), lambda bi, i: (bi, jnp.minimum((i + 1) * nh, last), col_blk))]


def _lru_kernel(u_ref, up_ref, un_ref, cw_ref, cb_ref, wr_ref, wi_ref, br_ref, bi_ref, lam_ref, h0_ref,
                o_ref, ext_scr, a_scr, b_scr, carry_scr, *, tm, reverse, blocks, halo):
    i, nt = pl.program_id(1), pl.num_programs(1)
    d = u_ref.shape[2]
    bw = d // blocks
    first = (i == 0)
    ti = (nt - 1 - i) if reverse else i

    @pl.when(first)
    def _():
        carry_scr[...] = h0_ref[0]

    ext_scr[0:halo] = jnp.where(ti > 0, up_ref[0], 0.0)
    ext_scr[halo:halo + tm] = u_ref[0]
    ext_scr[halo + tm:] = jnp.where(ti < nt - 1, un_ref[0], 0.0)
    width = cw_ref.shape[0]
    pad_l = width // 2
    u = jnp.broadcast_to(cb_ref[...], (tm, d))
    for k in range(width):
        u = u + cw_ref[k:k + 1, :] * ext_scr[pl.ds(halo - pad_l + k, tm), :]

    ub = u.astype(BF16)
    neg_c_sp = -LRU_C * _softplus(-lam_ref[...])
    for blk in range(blocks):
        sl = slice(blk * bw, (blk + 1) * bw)
        r = _sigmoid(_dot(ub[:, sl], wr_ref[blk]) + br_ref[:, sl])
        g = _sigmoid(_dot(ub[:, sl], wi_ref[blk]) + bi_ref[:, sl])
        log_a = neg_c_sp[:, sl] * r
        a = jnp.exp(log_a)
        a_scr[:, sl] = a
        b_scr[:, sl] = jnp.sqrt(1.0 - a * a) * (g * u[:, sl])

    sub = lax.broadcasted_iota(jnp.int32, (SUBLANE, d), 0)
    ngroups = tm // SUBLANE

    def group(gi, carry):
        r0 = pl.multiple_of(((ngroups - 1 - gi) if reverse else gi) * SUBLANE, SUBLANE)
        a8 = a_scr[pl.ds(r0, SUBLANE), :]
        b8 = b_scr[pl.ds(r0, SUBLANE), :]
        for sft in (1, 2, 4):
            if reverse:
                valid = sub < SUBLANE - sft
                a_sh = pltpu.roll(a8, SUBLANE - sft, 0)
                b_sh = pltpu.roll(b8, SUBLANE - sft, 0)
            else:
                valid = sub >= sft
                a_sh = pltpu.roll(a8, sft, 0)
                b_sh = pltpu.roll(b8, sft, 0)
            b8 = jnp.where(valid, a8 * b_sh + b8, b8)
            a8 = jnp.where(valid, a8 * a_sh, a8)
        h8 = a8 * carry + b8
        o_ref[0, pl.ds(r0, SUBLANE), :] = h8
        return h8[0:1, :] if reverse else h8[SUBLANE - 1:SUBLANE, :]

    carry_scr[...] = lax.fori_loop(0, ngroups, group, carry_scr[...])


def lru_scan(z, h0, conv_w, conv_b, w_r, w_i, b_r, b_i, lam, *, reverse, tm):
    b, s, r2 = z.shape
    r = r2 // 2
    tm = _pick(tm, s)
    nt = s // tm
    halo = SUBLANE
    nh = tm // halo
    last = s // halo - 1

    def ti(i):
        return nt - 1 - i if reverse else i

    vec = lambda: pl.BlockSpec((1, r), lambda bi, i: (0, 0))
    return pl.pallas_call(
        functools.partial(_lru_kernel, tm=tm, reverse=reverse, blocks=w_r.shape[0], halo=halo),
        grid=(b, nt),
        in_specs=[
            pl.BlockSpec((1, tm, r), lambda bi, i: (bi, ti(i), 1)),
            pl.BlockSpec((1, halo, r), lambda bi, i: (bi, jnp.maximum(ti(i) * nh - 1, 0), 1)),
            pl.BlockSpec((1, halo, r), lambda bi, i: (bi, jnp.minimum((ti(i) + 1) * nh, last), 1)),
            pl.BlockSpec(conv_w.shape, lambda bi, i: (0, 0)),
            vec(),
            pl.BlockSpec(w_r.shape, lambda bi, i: (0, 0, 0)),
            pl.BlockSpec(w_i.shape, lambda bi, i: (0, 0, 0)),
            vec(), vec(), vec(),
            pl.BlockSpec((1, 1, r), lambda bi, i: (bi, 0, 0)),
        ],
        out_specs=pl.BlockSpec((1, tm, r), lambda bi, i: (bi, ti(i), 0)),
        out_shape=jax.ShapeDtypeStruct((b, s, r), F32),
        scratch_shapes=[pltpu.VMEM((tm + 2 * halo, r), F32), pltpu.VMEM((tm, r), F32),
                        pltpu.VMEM((tm, r), F32), pltpu.VMEM((1, r), F32)],
        compiler_params=_cp("parallel", "arbitrary"),
        name="lru_scan",
    )(z, z, z, conv_w, conv_b.reshape(1, r), w_r, w_i, b_r.reshape(1, r), b_i.reshape(1, r),
      lam.reshape(1, r), h0.reshape(b, 1, r))


def _lru_out_prologue(pos, r0, rows, hf_ref, hb_ref, gate_ref):
    rs = slice(r0, r0 + rows)
    g = gate_ref[0, rs, :]
    gelu = 0.5 * g * (1.0 + jnp.tanh(0.7978845608028654 * (g + 0.044715 * (g * g * g))))
    return ((hf_ref[0, rs, :] + hb_ref[0, rs, :]) * gelu).astype(BF16)


def _row_spec(tm, cols, col_blk=0):
    return pl.BlockSpec((1, tm, cols), lambda bi, i: (bi, i, col_blk))


def _tiles(s):
    return min(ROW_TILE, s)


def _ptile(s):
    return min(PROJ_TILE, s)


def mlstm_mixer(xl, xc, g, mod_l, mod_c, w_in, w_gate, b_gate, head_g, w_out, need_ctx):
    b, s, d = xl.shape
    heads = ML_HEADS
    nv = w_out.shape[0]
    dv = nv // heads
    nqk = (w_in.shape[1] - 2 * nv) // 2
    dqk = nqk // heads
    n_in = w_in.shape[1]
    ngate = w_gate.shape[1]
    n_all = -(-(n_in + LANE) // MXU_WIDTH) * MXU_WIDTH
    n_pad = n_all - n_in - ngate
    w = jnp.concatenate([w_in, w_gate, jnp.zeros((d, n_pad), F32)], axis=1).astype(BF16)
    bias = jnp.concatenate([jnp.zeros((n_in,), F32), b_gate, jnp.zeros((n_pad,), F32)])
    cos, sin = _rope_tables(s, dqk)
    w_out_b = w_out.astype(BF16)

    def project(x, mod):
        z = norm_proj(x, g, mod, w, bias=bias)
        gates_t = jnp.swapaxes(z[:, :, n_in:n_in + ngate], 1, 2)
        return z, gates_t

    def scan(z, gt, state, reverse, rotary):
        sx = z.shape[1]
        return mlstm_scan(z, gt, cos[:sx], sin[:sx], state, reverse=reverse, rotary=rotary,
                          heads=heads, dqk=dqk, dv=dv, chunk=min(ML_CHUNK, sx))

    def readout(hf, hb, z, x, mod):
        tm = _ptile(x.shape[1])
        o_blk = (2 * nqk + nv) // nv
        specs = [_row_spec(tm, nv), _row_spec(tm, nv), _row_spec(tm, nv, o_blk),
                 pl.BlockSpec((1, nv), lambda bi, i: (0, 0))]
        pro = functools.partial(_mlstm_readout_prologue, heads=heads, dv=dv)
        return resid_proj(pro, [hf, hb, z, head_g.reshape(1, nv)], specs, w_out_b, x, mod, gate_row=2, tm=tm)

    zero = (jnp.zeros((b, heads, dqk, dv), F32), jnp.zeros((b, heads, 1, dqk), F32),
            jnp.zeros((b, heads, 1, LANE), F32))
    zc, gtc = project(xc, mod_c)
    zl, gtl = project(xl, mod_l)
    hc_f, st_f = scan(zc, gtc, zero, False, False)
    hc_b, st_b = scan(zc, gtc, zero, True, False)
    hl_f, _ = scan(zl, gtl, st_f, False, True)
    hl_b, _ = scan(zl, gtl, st_b, True, True)
    yl = readout(hl_f, hl_b, zl, xl, mod_l)
    yc = readout(hc_f, hc_b, zc, xc, mod_c) if need_ctx else None
    return yl, yc


def na_mixer(xl, xc, g, mod_l, mod_c, w_qkv, q_g, k_g, rpb, w_o, need_ctx):
    d = xl.shape[2]
    wb, wob = w_qkv.astype(BF16), w_o.astype(BF16)
    zl = norm_proj(xl, g, mod_l, wb)
    zc = norm_proj(xc, g, mod_c, wb)
    ol = na_attention(zl, zc, q_g, k_g, rpb, heads=NA_HEADS)

    def out(o, x, mod):
        tm = _ptile(x.shape[1])
        return resid_proj(_cast_prologue, [o], [_row_spec(tm, d)], wob, x, mod, gate_row=2, tm=tm)

    yl = out(ol, xl, mod_l)
    yc = out(ctx_attention(zc, q_g, k_g, heads=NA_HEADS), xc, mod_c) if need_ctx else None
    return yl, yc


def conformer_mixer(x, g, mod, w_pw1, dw, dw_b, ln_g, ln_b, w_pw2):
    b, s, d = x.shape
    width = dw.shape[0]
    halo = 16
    assert width // 2 <= halo
    tm = _ptile(s)
    a = norm_proj(x, g, mod, w_pw1.astype(BF16), glu=True)
    vec = lambda: pl.BlockSpec((1, d), lambda bi, i: (0, 0))
    specs = _halo_specs(tm, halo, s, d) + [pl.BlockSpec((width, d), lambda bi, i: (0, 0)),
                                           vec(), vec(), vec()]
    pro = functools.partial(_conformer_prologue, tm=tm, width=width, halo=halo)
    return resid_proj(pro, [a, a, a, dw, dw_b.reshape(1, d), ln_g.reshape(1, d), ln_b.reshape(1, d)],
                      specs, w_pw2.astype(BF16), x, mod, gate_row=2, tm=tm,
                      scratch=[pltpu.VMEM((tm + 2 * halo, d), F32), pltpu.VMEM((tm, d), F32),
                               pltpu.VMEM((4, _conv_rows(tm, width)[1], LANE), F32)])


def rglru_mixer(xl, xc, g, mod_l, mod_c, w_in, conv_w, conv_b, w_gate, b_gate, lam, w_out, need_ctx):
    b, s, d = xl.shape
    r = w_out.shape[0]
    wb, wob, wgb = w_in.astype(BF16), w_out.astype(BF16), w_gate.astype(BF16)
    zl = norm_proj(xl, g, mod_l, wb)
    zc = norm_proj(xc, g, mod_c, wb)

    def scan(z, h0, reverse):
        k = 2 if reverse else 0
        return lru_scan(z, h0, conv_w, conv_b, wgb[k], wgb[k + 1], b_gate[k], b_gate[k + 1],
                        lam[1 if reverse else 0], reverse=reverse, tm=min(LRU_TILE, z.shape[1]))

    h0 = jnp.zeros((b, r), F32)
    hc_f = scan(zc, h0, False)
    hc_b = scan(zc, h0, True)
    hl_f = scan(zl, hc_f[:, -1], False)
    hl_b = scan(zl, hc_b[:, 0], True)

    def out(hf, hb, z, x, mod):
        tm = _ptile(x.shape[1])
        specs = [_row_spec(tm, r), _row_spec(tm, r), _row_spec(tm, r, 0)]
        return resid_proj(_lru_out_prologue, [hf, hb, z], specs, wob, x, mod, gate_row=2, tm=tm)

    yl = out(hl_f, hl_b, zl, xl, mod_l)
    yc = out(hc_f, hc_b, zc, xc, mod_c) if need_ctx else None
    return yl, yc


def kernel(x, c, ctx, c_ctx, norm_mix, norm_ffn, ada_w, ada_b, ml_w_in, ml_w_gate, ml_b_gate, ml_head_g, ml_w_out, na_w_qkv, na_q_g, na_k_g, na_rpb, na_w_o, cv_w_pw1, cv_dw, cv_dw_b, cv_ln_g, cv_ln_b, cv_w_pw2, lr_w_in, lr_conv, lr_conv_b, lr_w_gate, lr_b_gate, lr_lambda, lr_w_out, ffn_w_gu, ffn_conv, ffn_w_down):
    b, s, d = x.shape
    depth = ada_w.shape[0]
    n_mixers = 4
    cc_rows = -(-(b + 1) // SUBLANE) * SUBLANE
    cc = jnp.concatenate([c, c_ctx[None, :], jnp.zeros((cc_rows - b - 1, d), F32)], axis=0)
    mods = ada_modulation(cc, ada_w, ada_b).reshape(depth, cc_rows, 6, d)
    mods = jnp.pad(mods, ((0, 0), (0, 0), (0, MOD_ROWS - 6), (0, 0)))

    xl, xc = x, ctx
    for i in range(depth):
        kind, j = i % n_mixers, i // n_mixers
        need_ctx = i < depth - 1
        mod_l, mod_c = mods[i, :b], mods[i, b:b + 1]
        g = norm_mix[i]
        if kind == 0:
            xl, xc_new = mlstm_mixer(xl, xc, g, mod_l, mod_c, ml_w_in[j], ml_w_gate[j], ml_b_gate[j],
                                     ml_head_g[j], ml_w_out[j], need_ctx)
        elif kind == 1:
            xl, xc_new = na_mixer(xl, xc, g, mod_l, mod_c, na_w_qkv[j], na_q_g[j], na_k_g[j],
                                  na_rpb[j], na_w_o[j], need_ctx)
        elif kind == 2:
            cv = (cv_w_pw1[j], cv_dw[j], cv_dw_b[j], cv_ln_g[j], cv_ln_b[j], cv_w_pw2[j])
            xl, xc_new = conformer_mixer(xl, g, mod_l, *cv), None
            if need_ctx:
                xc_new = conformer_mixer(xc, g, mod_c, *cv)
        else:
            xl, xc_new = rglru_mixer(xl, xc, g, mod_l, mod_c, lr_w_in[j], lr_conv[j], lr_conv_b[j],
                                     lr_w_gate[j], lr_b_gate[j], lr_lambda[j], lr_w_out[j], need_ctx)
        wgu, wdn = ffn_w_gu[i].astype(BF16), ffn_w_down[i].astype(BF16)
        fc = 512 if wdn.shape[0] % 512 == 0 else LANE
        xl = conv_ffn(xl, norm_ffn[i], mod_l, wgu, ffn_conv[i], wdn, tm=_tiles(s), fc=fc)
        if need_ctx:
            xc = conv_ffn(xc_new, norm_ffn[i], mod_c, wgu, ffn_conv[i], wdn, tm=_tiles(xc.shape[1]), fc=fc)
    return xl
```

```python
import functools

import numpy as np
import jax
import jax.numpy as jnp
from jax import lax
from jax.experimental import pallas as pl
from jax.experimental.pallas import tpu as pltpu

F32 = jnp.float32
BF16 = jnp.bfloat16

EPS = 1e-6
ROPE_THETA = 10000.0
GRID_W = 64
ML_HEADS = 8
ML_CHUNK = 256
NA_HEADS = 16
WIN_ROWS = 8
WIN_COLS = 16
NA_QROWS = 4
NA_HEADS_PER_STEP = 2
NA_QBLOCKS_PER_STEP = 8
LRU_BLOCKS = 8
LRU_C = 8.0
NEG = -1e30

V7X_VMEM_LIMIT = 56 * 1024 * 1024
LANE = 128
SUBLANE = 8
MOD_ROWS = 8
ROW_TILE = 512
PROJ_TILE = 256
PROJ_SUB = 128
MXU_WIDTH = 256
LRU_TILE = 256


def _cp(*sem):
    return pltpu.CompilerParams(dimension_semantics=sem, vmem_limit_bytes=V7X_VMEM_LIMIT)


def _dot(a, b):
    return jnp.dot(a, b, preferred_element_type=F32)


def _dot_nt(a, b):
    return lax.dot_general(a, b, (((1,), (1,)), ((), ())), preferred_element_type=F32)


def _sigmoid(x):
    return 1.0 / (1.0 + jnp.exp(-x))


def _log_sigmoid(x):
    return jnp.minimum(x, 0.0) - jnp.log(1.0 + jnp.exp(-jnp.abs(x)))


def _softplus(x):
    return jnp.maximum(x, 0.0) + jnp.log(1.0 + jnp.exp(-jnp.abs(x)))


def _rms(x, g):
    return x * lax.rsqrt(jnp.mean(x * x, axis=-1, keepdims=True) + EPS) * g


def _norm_mod(x, g, shift, scale):
    return _rms(x, g) * (1.0 + scale) + shift


def _pick(tile, n):
    assert n % tile == 0, (tile, n)
    return tile


def _ada_kernel(c_ref, w_ref, b_ref, o_ref):
    c = c_ref[...]
    s = (c * _sigmoid(c)).astype(BF16)
    o_ref[0] = _dot(s, w_ref[0].astype(BF16)) + b_ref[0]


def ada_modulation(cc, ada_w, ada_b):
    depth, d, n = ada_w.shape
    rows = cc.shape[0]
    tn = _pick(1024 if n % 1024 == 0 else n, n)
    return pl.pallas_call(
        _ada_kernel,
        grid=(depth, n // tn),
        in_specs=[pl.BlockSpec((rows, d), lambda l, j: (0, 0)),
                  pl.BlockSpec((1, d, tn), lambda l, j: (l, 0, j)),
                  pl.BlockSpec((1, 1, tn), lambda l, j: (l, 0, j))],
        out_specs=pl.BlockSpec((1, rows, tn), lambda l, j: (l, 0, j)),
        out_shape=jax.ShapeDtypeStruct((depth, rows, n), F32),
        compiler_params=_cp("parallel", "parallel"),
        name="ada_modulation",
    )(cc, ada_w, ada_b.reshape(depth, 1, n))


def _col_chunk(n):
    for c in (1536, 1280, 1024, 768, 512, 256, 128):
        if n % c == 0:
            return c
    return n


def _resident(shape):
    return pl.BlockSpec(shape, lambda *_: (0,) * len(shape), pipeline_mode=pl.Buffered(1))


def _proj_kernel(x_ref, g_ref, mod_ref, w_ref, *rest, glu, bias, sub):
    if bias:
        b_ref, o_ref = rest
    else:
        (o_ref,) = rest
    tm, n = o_ref.shape[1], o_ref.shape[2]
    nchunk = _col_chunk(n)
    shift = mod_ref[0, 0:1, :]
    gain = g_ref[...] * (1.0 + mod_ref[0, 1:2, :])
    for r0 in range(0, tm, sub):
        x = x_ref[0, r0:r0 + sub, :]
        h = (x * lax.rsqrt(jnp.mean(x * x, axis=-1, keepdims=True) + EPS) * gain + shift).astype(BF16)
        for c0 in range(0, n, nchunk):
            y = _dot(h, w_ref[:, c0:c0 + nchunk])
            if glu:
                y = y * _sigmoid(_dot(h, w_ref[:, n + c0:n + c0 + nchunk]))
            if bias:
                y = y + b_ref[:, c0:c0 + nchunk]
            o_ref[0, r0:r0 + sub, c0:c0 + nchunk] = y


def norm_proj(x, g, mod, w, *, bias=None, glu=False):
    b, s, d = x.shape
    n = w.shape[1] // 2 if glu else w.shape[1]
    tm = min(PROJ_TILE, s)
    sub = min(PROJ_SUB, tm)
    assert s % tm == 0 and tm % sub == 0
    per_batch = mod.shape[0] != 1
    in_specs = [pl.BlockSpec((1, tm, d), lambda bi, i: (bi, i, 0)),
                pl.BlockSpec((1, d), lambda bi, i: (0, 0)),
                pl.BlockSpec((1, MOD_ROWS, d), lambda bi, i: (bi if per_batch else 0, 0, 0)),
                _resident(w.shape)]
    args = [x, g.reshape(1, d), mod, w]
    if bias is not None:
        in_specs.append(pl.BlockSpec((1, n), lambda bi, i: (0, 0)))
        args.append(bias.reshape(1, n))
    return pl.pallas_call(
        functools.partial(_proj_kernel, glu=glu, bias=bias is not None, sub=sub),
        grid=(b, s // tm),
        in_specs=in_specs,
        out_specs=pl.BlockSpec((1, tm, n), lambda bi, i: (bi, i, 0)),
        out_shape=jax.ShapeDtypeStruct((b, s, n), F32),
        compiler_params=_cp("parallel", "parallel"),
        name="norm_proj",
    )(*args)


def _resid_kernel(*refs, prologue, n_in, gate_row, sub):
    in_refs = refs[:n_in]
    w_ref, x_ref, mod_ref, o_ref, *scr = refs[n_in:]
    pos = (pl.program_id(1), pl.num_programs(1))
    tm, n = x_ref.shape[1], x_ref.shape[2]
    nchunk = _col_chunk(n)
    gate = mod_ref[0, gate_row:gate_row + 1, :]
    for r0 in range(0, tm, sub):
        h = prologue(pos, r0, sub, *in_refs, *scr)
        for c0 in range(0, n, nchunk):
            y = _dot(h, w_ref[:, c0:c0 + nchunk])
            o_ref[0, r0:r0 + sub, c0:c0 + nchunk] = (x_ref[0, r0:r0 + sub, c0:c0 + nchunk]
                                                    + gate[:, c0:c0 + nchunk] * y)


def resid_proj(prologue, ins, in_specs, w, x, mod, *, gate_row, tm, scratch=()):
    b, s, n = x.shape
    sub = min(PROJ_SUB, tm)
    assert s % tm == 0 and tm % sub == 0
    per_batch = mod.shape[0] != 1
    specs = list(in_specs) + [
        _resident(w.shape),
        pl.BlockSpec((1, tm, n), lambda bi, i: (bi, i, 0)),
        pl.BlockSpec((1, MOD_ROWS, n), lambda bi, i: (bi if per_batch else 0, 0, 0))]
    return pl.pallas_call(
        functools.partial(_resid_kernel, prologue=prologue, n_in=len(ins), gate_row=gate_row, sub=sub),
        grid=(b, s // tm),
        in_specs=specs,
        out_specs=pl.BlockSpec((1, tm, n), lambda bi, i: (bi, i, 0)),
        out_shape=jax.ShapeDtypeStruct((b, s, n), F32),
        scratch_shapes=list(scratch),
        compiler_params=_cp("parallel", "parallel"),
        name="resid_proj",
    )(*ins, w, x, mod)


def _ffn_kernel(x_ref, xp_ref, xn_ref, g_ref, mod_ref, wg_ref, wu_ref, cw_ref, wd_ref, o_ref,
                h_scr, g_scr, acc_scr, *, tm, seq):
    i, j = pl.program_id(1), pl.program_id(2)
    nt, nf = pl.num_programs(1), pl.num_programs(2)
    hb = SUBLANE

    @pl.when(j == 0)
    def _():
        g, shift, scale = g_ref[...], mod_ref[0, 3:4, :], mod_ref[0, 4:5, :]
        h_scr[0:tm] = _norm_mod(x_ref[0], g, shift, scale).astype(BF16)
        halo = jnp.concatenate([xp_ref[0], xn_ref[0]], axis=0)
        h_scr[tm:] = _norm_mod(halo, g, shift, scale).astype(BF16)
        acc_scr[...] = jnp.zeros_like(acc_scr)

    ga = _dot(h_scr[...], wg_ref[...])
    uu = _dot(h_scr[0:tm], wu_ref[...])
    gg = ga[0:tm]
    g_scr[0:hb] = jnp.where(i > 0, ga[tm:tm + hb], 0.0)
    g_scr[hb:hb + tm] = gg
    g_scr[hb + tm:] = jnp.where(i < nt - 1, ga[tm + hb:], 0.0)
    cw = cw_ref[...]
    g_prev, g_next = g_scr[pl.ds(hb - 1, tm), :], g_scr[pl.ds(hb + 1, tm), :]
    if seq < tm:
        pos = lax.broadcasted_iota(jnp.int32, (tm, 1), 0) % seq
        g_prev = jnp.where(pos > 0, g_prev, 0.0)
        g_next = jnp.where(pos < seq - 1, g_next, 0.0)
    gc = cw[0:1] * g_prev + cw[1:2] * gg + cw[2:3] * g_next
    act = (gc * _sigmoid(gc) * uu).astype(BF16)
    acc_scr[...] += _dot(act, wd_ref[...])

    @pl.when(j == nf - 1)
    def _():
        o_ref[0] = x_ref[0] + mod_ref[0, 5:6, :] * acc_scr[...]


def conv_ffn(x, g, mod, w_gu, conv_w, w_down, *, tm, fc):
    seq = x.shape[1]
    per_batch = mod.shape[0] != 1
    if not per_batch and tm % seq == 0:
        out = _conv_ffn(x.reshape(1, -1, x.shape[2]), g, mod, w_gu, conv_w, w_down, tm=tm, fc=fc, seq=seq)
        return out.reshape(x.shape)
    return _conv_ffn(x, g, mod, w_gu, conv_w, w_down, tm=tm, fc=fc, seq=seq)


def _conv_ffn(x, g, mod, w_gu, conv_w, w_down, *, tm, fc, seq):
    b, s, d = x.shape
    f = w_down.shape[0]
    tm, fc = _pick(min(tm, s), s), _pick(fc, f)
    assert seq == s or tm % seq == 0
    per_batch = mod.shape[0] != 1
    nh = tm // SUBLANE
    last_h = s // SUBLANE - 1
    return pl.pallas_call(
        functools.partial(_ffn_kernel, tm=tm, seq=seq),
        grid=(b, s // tm, f // fc),
        in_specs=[
            pl.BlockSpec((1, tm, d), lambda bi, i, j: (bi, i, 0)),
            pl.BlockSpec((1, SUBLANE, d), lambda bi, i, j: (bi, jnp.maximum(i * nh - 1, 0), 0)),
            pl.BlockSpec((1, SUBLANE, d), lambda bi, i, j: (bi, jnp.minimum((i + 1) * nh, last_h), 0)),
            pl.BlockSpec((1, d), lambda bi, i, j: (0, 0)),
            pl.BlockSpec((1, MOD_ROWS, d), lambda bi, i, j: (bi if per_batch else 0, 0, 0)),
            pl.BlockSpec((d, fc), lambda bi, i, j: (0, j)),
            pl.BlockSpec((d, fc), lambda bi, i, j: (0, j + f // fc)),
            pl.BlockSpec((conv_w.shape[0], fc), lambda bi, i, j: (0, j)),
            pl.BlockSpec((fc, d), lambda bi, i, j: (j, 0)),
        ],
        out_specs=pl.BlockSpec((1, tm, d), lambda bi, i, j: (bi, i, 0)),
        out_shape=jax.ShapeDtypeStruct((b, s, d), F32),
        scratch_shapes=[pltpu.VMEM((tm + 2 * SUBLANE, d), BF16),
                        pltpu.VMEM((tm + 2 * SUBLANE, fc), F32),
                        pltpu.VMEM((tm, d), F32)],
        compiler_params=_cp("parallel", "parallel", "arbitrary"),
        name="conv_ffn",
    )(x, x, x, g.reshape(1, d), mod, w_gu, w_gu, conv_w, w_down)


def _split3(x):
    hi = x.astype(BF16)
    r = x - hi.astype(F32)
    mid = r.astype(BF16)
    lo = (r - mid.astype(F32)).astype(BF16)
    return hi, mid, lo


def _rope(x, cos, sin_signed):
    half = x.shape[-1] // 4
    lane = lax.broadcasted_iota(jnp.int32, x.shape, 1)
    partner = jnp.where((lane % (2 * half)) < half,
                        pltpu.roll(x, x.shape[-1] - half, 1), pltpu.roll(x, half, 1))
    return x * cos + partner * sin_signed


def _mlstm_kernel(q_ref, k_ref, v_ref, gc_ref, gr_ref, cos_ref, sin_ref, c0_ref, n0_ref, m0_ref,
                  h_ref, c_out, n_out, m_out, c_scr, n_scr, m_scr, *, reverse, rotary, heads, dqk, dv):
    c = pl.program_id(1)
    nc = pl.num_programs(1)
    L = q_ref.shape[1]

    @pl.when(c == 0)
    def _():
        c_scr[...] = c0_ref[0]
        n_scr[...] = n0_ref[0]
        m_scr[...] = m0_ref[0]

    off_i = 2 * heads if reverse else 0
    off_f = off_i + heads
    gcol = gc_ref[0]
    grow = gr_ref[0]
    lane = lax.broadcasted_iota(jnp.int32, gcol.shape, 1)

    row = lax.broadcasted_iota(jnp.int32, (L, L), 0)
    col = lax.broadcasted_iota(jnp.int32, (L, L), 1)
    mask = (col >= row) if reverse else (col <= row)
    mask_t = (row >= col) if reverse else (row <= col)
    tri = jnp.where(mask, 1.0, 0.0).astype(BF16)
    tri_t = jnp.where(mask_t, 1.0, 0.0).astype(BF16)
    lf_rows = _log_sigmoid(grow)
    b_all = sum(_dot(tri, p) for p in _split3(_log_sigmoid(gcol)))
    b_rows = sum(_dot(p, tri_t) for p in _split3(lf_rows))

    for hd in range(heads):
        qs, vs = slice(hd * dqk, (hd + 1) * dqk), slice(hd * dv, (hd + 1) * dv)
        q = q_ref[0, :, qs] * (dqk ** -0.5)
        k = k_ref[0, :, qs]
        if rotary:
            q = _rope(q, cos_ref[...], sin_ref[...])
            k = _rope(k, cos_ref[...], sin_ref[...])
        qb, kb, vb = q.astype(BF16), k.astype(BF16), v_ref[0, :, vs].astype(BF16)

        li_col = jnp.sum(jnp.where(lane == off_i + hd, gcol, 0.0), axis=-1, keepdims=True)
        b_col = jnp.sum(jnp.where(lane == off_f + hd, b_all, 0.0), axis=-1, keepdims=True)
        li_row = grow[off_i + hd:off_i + hd + 1]
        b_row = b_rows[off_f + hd:off_f + hd + 1]
        total = jnp.sum(lf_rows[off_f + hd:off_f + hd + 1], axis=-1, keepdims=True)

        m_prev = m_scr[hd, 0:1, 0:1]
        dmat = jnp.where(mask, b_col - b_row + li_row, NEG)
        inter = b_col + m_prev
        m_t = jnp.maximum(inter, jnp.max(dmat, axis=-1, keepdims=True))
        w_inter = jnp.exp(inter - m_t)
        s_qk = _dot_nt(qb, kb) * jnp.exp(dmat - m_t)
        num = w_inter * _dot(qb, c_scr[hd].astype(BF16)) + _dot(s_qk.astype(BF16), vb)
        den = (w_inter * jnp.sum(q * n_scr[hd], axis=-1, keepdims=True)
               + jnp.sum(s_qk, axis=-1, keepdims=True))
        h_ref[0, :, vs] = num / jnp.maximum(jnp.abs(den), jnp.exp(-m_t))

        w_col = total - b_col + li_col
        m_new = jnp.maximum(total + m_prev, jnp.max(w_col, axis=0, keepdims=True))
        decay = jnp.exp(total + m_prev - m_new)
        ek = k * jnp.exp(w_col - m_new)
        c_scr[hd] = decay * c_scr[hd] + _dot(ek.T.astype(BF16), vb)
        n_scr[hd] = decay * n_scr[hd] + jnp.sum(ek, axis=0, keepdims=True)
        m_scr[hd] = jnp.broadcast_to(m_new, (1, m_scr.shape[2]))

    @pl.when(c == nc - 1)
    def _():
        c_out[0] = c_scr[...]
        n_out[0] = n_scr[...]
        m_out[0] = m_scr[...]


def mlstm_scan(z, gates_t, cos, sin, state, *, reverse, rotary, heads, dqk, dv, chunk):
    b, s, _ = z.shape
    L = _pick(chunk, s)
    nc = s // L
    nqk, nv = heads * dqk, heads * dv
    assert (2 * nqk) % nv == 0
    gate_blk = (2 * nqk + 2 * nv) // LANE

    def ci(c):
        return nc - 1 - c if reverse else c

    c0, n0, m0 = state
    st_specs = [pl.BlockSpec((1, heads, dqk, dv), lambda bi, c: (bi, 0, 0, 0)),
                pl.BlockSpec((1, heads, 1, dqk), lambda bi, c: (bi, 0, 0, 0)),
                pl.BlockSpec((1, heads, 1, LANE), lambda bi, c: (bi, 0, 0, 0))]
    h, c1, n1, m1 = pl.pallas_call(
        functools.partial(_mlstm_kernel, reverse=reverse, rotary=rotary, heads=heads, dqk=dqk, dv=dv),
        grid=(b, nc),
        in_specs=[
            pl.BlockSpec((1, L, nqk), lambda bi, c: (bi, ci(c), 0)),
            pl.BlockSpec((1, L, nqk), lambda bi, c: (bi, ci(c), 1)),
            pl.BlockSpec((1, L, nv), lambda bi, c: (bi, ci(c), 2 * nqk // nv)),
            pl.BlockSpec((1, L, LANE), lambda bi, c: (bi, ci(c), gate_blk)),
            pl.BlockSpec((1, 4 * heads, L), lambda bi, c: (bi, 0, ci(c))),
            pl.BlockSpec((L, dqk), lambda bi, c: (ci(c), 0)),
            pl.BlockSpec((L, dqk), lambda bi, c: (ci(c), 0)),
        ] + st_specs,
        out_specs=[pl.BlockSpec((1, L, nv), lambda bi, c: (bi, ci(c), 0))] + st_specs,
        out_shape=[jax.ShapeDtypeStruct((b, s, nv), F32),
                   jax.ShapeDtypeStruct(c0.shape, F32),
                   jax.ShapeDtypeStruct(n0.shape, F32),
                   jax.ShapeDtypeStruct(m0.shape, F32)],
        scratch_shapes=[pltpu.VMEM((heads, dqk, dv), F32), pltpu.VMEM((heads, 1, dqk), F32),
                        pltpu.VMEM((heads, 1, LANE), F32)],
        compiler_params=_cp("parallel", "arbitrary"),
        name="mlstm_scan",
    )(z, z, z, z, gates_t, cos, sin, c0, n0, m0)
    return h, (c1, n1, m1)


def _rope_tables(s, d):
    half, quarter = d // 2, d // 4
    freqs = ROPE_THETA ** (-jnp.arange(0, half, 2, dtype=F32) / half)
    t = jnp.arange(s)
    ang_r = (t // GRID_W).astype(F32)[:, None] * freqs[None, :]
    ang_c = (t % GRID_W).astype(F32)[:, None] * freqs[None, :]
    cos = jnp.concatenate([jnp.cos(ang_r)] * 2 + [jnp.cos(ang_c)] * 2, axis=-1)
    sin = jnp.concatenate([-jnp.sin(ang_r), jnp.sin(ang_r), -jnp.sin(ang_c), jnp.sin(ang_c)], axis=-1)
    del quarter
    return cos, sin


def _mlstm_readout_prologue(pos, r0, rows, hf_ref, hb_ref, o_ref, g_ref, *, heads, dv):
    rs = slice(r0, r0 + rows)
    parts = []
    for hd in range(heads):
        sl = slice(hd * dv, (hd + 1) * dv)
        hs = hf_ref[0, rs, sl] + hb_ref[0, rs, sl]
        parts.append((_rms(hs, g_ref[:, sl]) * _sigmoid(o_ref[0, rs, sl])).astype(BF16))
    return jnp.concatenate(parts, axis=-1)


def _na_kernel(q_ref, k_ref, v_ref, kc_ref, vc_ref, qg_ref, kg_ref, bias_ref, o_ref,
               kn_scr, vn_scr, kcn_scr, vcn_scr, *, nkeys, kstep, half, kr_max, hps, dh, rps):
    nb = pl.num_programs(2) * rps
    tq = q_ref.shape[1] // rps

    @pl.when(pl.program_id(2) == 0)
    def _():
        for hh in range(hps):
            sl = slice(hh * dh, (hh + 1) * dh)
            kn_scr[hh] = _rms(k_ref[0, :, sl], kg_ref[...]).astype(BF16)
            kcn_scr[hh] = _rms(kc_ref[0, :, sl], kg_ref[...]).astype(BF16)
            vn_scr[hh, :, 0:dh] = v_ref[0, :, sl].astype(BF16)
            vn_scr[hh, :, dh:] = jnp.ones((vn_scr.shape[1], dh), BF16)
            vcn_scr[hh, :, 0:dh] = vc_ref[0, :, sl].astype(BF16)
            vcn_scr[hh, :, dh:] = jnp.ones((vcn_scr.shape[1], dh), BF16)

    for rr in range(rps):
        rb = pl.program_id(2) * rps + rr
        rows = slice(rr * tq, (rr + 1) * tq)
        kr0 = jnp.clip(rb * (tq // kstep) - half, 0, kr_max)
        start = pl.multiple_of(kr0 * kstep, kstep)
        cls = jnp.where(rb == 0, 0, jnp.where(rb == nb - 1, 2, 1))
        for hh in range(hps):
            sl = slice(hh * dh, (hh + 1) * dh)
            qn = (_rms(q_ref[0, rows, sl], qg_ref[...]) * (dh ** -0.5)).astype(BF16)
            s_loc = _dot_nt(qn, kn_scr[hh, pl.ds(start, nkeys), :]) + bias_ref[hh, cls]
            s_ctx = _dot_nt(qn, kcn_scr[hh])
            m = jnp.maximum(jnp.max(s_loc, axis=-1, keepdims=True), jnp.max(s_ctx, axis=-1, keepdims=True))
            p_loc = jnp.exp(s_loc - m).astype(BF16)
            p_ctx = jnp.exp(s_ctx - m).astype(BF16)
            ol = _dot(p_loc, vn_scr[hh, pl.ds(start, nkeys), :]) + _dot(p_ctx, vcn_scr[hh])
            o_ref[0, rows, sl] = ol[:, 0:dh] / ol[:, dh:dh + 1]


def _na_bias_tables(rows):
    W, R, half = GRID_W, NA_QROWS, WIN_ROWS // 2
    win_r = min(WIN_ROWS, rows)
    U = R + win_r - 1
    nb = rows // R
    kr_max = rows - U
    assert nb >= 3 and R >= half and kr_max >= 0
    ri, kj = np.arange(R)[:, None], np.arange(U)[None, :]
    per_block = []
    for rb in range(nb):
        r = rb * R + ri
        kr = np.clip(rb * R - half, 0, kr_max) + kj
        rs = np.clip(r - half, 0, rows - win_r)
        ok = (kr >= rs) & (kr < rs + win_r)
        dr = np.clip(kr - r + WIN_ROWS - 1, 0, 2 * WIN_ROWS - 2)
        per_block.append((ok, dr))
    for rb in range(2, nb - 1):
        assert all(np.array_equal(a, b_) for a, b_ in zip(per_block[1], per_block[rb]))
    return [per_block[0], per_block[1], per_block[nb - 1]], U, kr_max


def _na_bias_kernel(rpb_ref, o_ref, t_scr, *, classes):
    h = pl.program_id(0)
    W = t_scr.shape[1]
    ndr, ndc = 2 * WIN_ROWS - 1, 2 * WIN_COLS - 1
    c = lax.broadcasted_iota(jnp.int32, (W, W), 0)
    kc = lax.broadcasted_iota(jnp.int32, (W, W), 1)
    cstart = jnp.clip(c - WIN_COLS // 2, 0, W - WIN_COLS)
    dcidx = jnp.clip(kc - c + WIN_COLS - 1, 0, ndc - 1)
    for dr in range(ndr):
        t = jnp.full((W, W), NEG, F32)
        for dc in range(ndc):
            t = jnp.where(dcidx == dc, rpb_ref[h, dr * ndc + dc], t)
        t = jnp.where(kc >= cstart, t, NEG)
        t_scr[dr] = jnp.where(kc < cstart + WIN_COLS, t, NEG)
    o_ref[...] = jnp.full(o_ref.shape, NEG, F32)
    for cls, (ok, dr) in enumerate(classes):
        for ri in range(ok.shape[0]):
            for kj in range(ok.shape[1]):
                if ok[ri, kj]:
                    o_ref[0, cls, ri * W:(ri + 1) * W, kj * W:(kj + 1) * W] = t_scr[int(dr[ri, kj])]


def na_bias_table(rpb, classes, tq, nkeys):
    heads = rpb.shape[0]
    return pl.pallas_call(
        functools.partial(_na_bias_kernel, classes=classes),
        grid=(heads,),
        in_specs=[pl.BlockSpec(memory_space=pltpu.SMEM)],
        out_specs=pl.BlockSpec((1, len(classes), tq, nkeys), lambda h: (h, 0, 0, 0)),
        out_shape=jax.ShapeDtypeStruct((heads, len(classes), tq, nkeys), F32),
        scratch_shapes=[pltpu.VMEM((2 * WIN_ROWS - 1, GRID_W, GRID_W), F32)],
        compiler_params=_cp("parallel"),
        name="na_bias_table",
    )(rpb.reshape(heads, -1))


def na_attention(z, zc, q_g, k_g, rpb, *, heads):
    b, s, n3 = z.shape
    dh = n3 // (3 * heads)
    cl = zc.shape[1]
    rows = s // GRID_W
    classes, U, kr_max = _na_bias_tables(rows)
    tq, nkeys = NA_QROWS * GRID_W, U * GRID_W
    bias = na_bias_table(rpb, classes, tq, nkeys)
    hps, rps = NA_HEADS_PER_STEP, NA_QBLOCKS_PER_STEP
    hg, w = heads // hps, hps * dh
    assert (s // tq) % rps == 0
    return pl.pallas_call(
        functools.partial(_na_kernel, nkeys=nkeys, kstep=GRID_W, half=WIN_ROWS // 2, kr_max=kr_max,
                          hps=hps, dh=dh, rps=rps),
        grid=(b, hg, s // (tq * rps)),
        in_specs=[
            pl.BlockSpec((1, tq * rps, w), lambda bi, h, r: (bi, r, h)),
            pl.BlockSpec((1, s, w), lambda bi, h, r: (bi, 0, hg + h)),
            pl.BlockSpec((1, s, w), lambda bi, h, r: (bi, 0, 2 * hg + h)),
            pl.BlockSpec((1, cl, w), lambda bi, h, r: (bi, 0, hg + h)),
            pl.BlockSpec((1, cl, w), lambda bi, h, r: (bi, 0, 2 * hg + h)),
            pl.BlockSpec((1, dh), lambda bi, h, r: (0, 0)),
            pl.BlockSpec((1, dh), lambda bi, h, r: (0, 0)),
            pl.BlockSpec((hps, 3, tq, nkeys), lambda bi, h, r: (h, 0, 0, 0)),
        ],
        out_specs=pl.BlockSpec((1, tq * rps, w), lambda bi, h, r: (bi, r, h)),
        out_shape=jax.ShapeDtypeStruct((b, s, heads * dh), F32),
        scratch_shapes=[pltpu.VMEM((hps, s, dh), BF16), pltpu.VMEM((hps, s, 2 * dh), BF16),
                        pltpu.VMEM((hps, cl, dh), BF16), pltpu.VMEM((hps, cl, 2 * dh), BF16)],
        compiler_params=_cp("parallel", "parallel", "arbitrary"),
        name="na_attention",
    )(z, z, z, zc, zc, q_g.reshape(1, dh), k_g.reshape(1, dh), bias)


def _ctx_attn_kernel(q_ref, k_ref, v_ref, qg_ref, kg_ref, o_ref):
    dh = q_ref.shape[2]
    qn = (_rms(q_ref[0], qg_ref[...]) * (dh ** -0.5)).astype(BF16)
    kn = _rms(k_ref[0], kg_ref[...]).astype(BF16)
    s = _dot_nt(qn, kn)
    p = jnp.exp(s - jnp.max(s, axis=-1, keepdims=True))
    l = jnp.sum(p, axis=-1, keepdims=True)
    o_ref[0] = _dot(p.astype(BF16), v_ref[0].astype(BF16)) / l


def ctx_attention(zc, q_g, k_g, *, heads):
    b, cl, n3 = zc.shape
    dh = n3 // (3 * heads)
    return pl.pallas_call(
        _ctx_attn_kernel,
        grid=(b, heads),
        in_specs=[pl.BlockSpec((1, cl, dh), lambda bi, h: (bi, 0, h)),
                  pl.BlockSpec((1, cl, dh), lambda bi, h: (bi, 0, heads + h)),
                  pl.BlockSpec((1, cl, dh), lambda bi, h: (bi, 0, 2 * heads + h)),
                  pl.BlockSpec((1, dh), lambda bi, h: (0, 0)),
                  pl.BlockSpec((1, dh), lambda bi, h: (0, 0))],
        out_specs=pl.BlockSpec((1, cl, dh), lambda bi, h: (bi, 0, h)),
        out_shape=jax.ShapeDtypeStruct((b, cl, heads * dh), F32),
        compiler_params=_cp("parallel", "parallel"),
        name="ctx_attention",
    )(zc, zc, zc, q_g.reshape(1, dh), k_g.reshape(1, dh))


def _cast_prologue(pos, r0, rows, a_ref):
    return a_ref[0, r0:r0 + rows, :].astype(BF16)


def _conv_rows(tm, width):
    rsub = min(tm, PROJ_SUB)
    return rsub, rsub + (-(-width // SUBLANE) - 1) * SUBLANE


def _conformer_prologue(pos, r0, rows, a_ref, ap_ref, an_ref, dw_ref, dwb_ref, lg_ref, lb_ref, ext_scr, y_scr,
                        sh_scr, *, tm, width, halo):
    i, nt = pos
    d = a_ref.shape[2]
    pad = width // 2
    if r0 == 0:
        ext_scr[0:halo] = jnp.where(i > 0, ap_ref[0], 0.0)
        ext_scr[halo:halo + tm] = a_ref[0]
        ext_scr[halo + tm:] = jnp.where(i < nt - 1, an_ref[0], 0.0)

    rsub, span = _conv_rows(tm, width)
    assert rows == rsub
    n_a = -(-width // SUBLANE)

    def col_block(cb, carry):
        cs = pl.ds(pl.multiple_of(cb * LANE, LANE), LANE)
        acc = jnp.broadcast_to(dwb_ref[:, cs], (rsub, LANE))
        for b in range(SUBLANE):
            slot = b % sh_scr.shape[0]
            sh_scr[slot] = ext_scr[pl.ds(halo - pad + r0 + b, span), cs]
            for a in range(n_a):
                k = a * SUBLANE + b
                if k < width:
                    acc = acc + dw_ref[k:k + 1, cs] * sh_scr[slot, a * SUBLANE:a * SUBLANE + rsub, :]
        y_scr[r0:r0 + rsub, cs] = acc
        return carry

    lax.fori_loop(0, d // LANE, col_block, 0)
    y = y_scr[r0:r0 + rsub, :]
    mu = jnp.mean(y, axis=-1, keepdims=True)
    var = jnp.mean(jnp.square(y - mu), axis=-1, keepdims=True)
    yn = (y - mu) * lax.rsqrt(var + EPS) * lg_ref[...] + lb_ref[...]
    return (yn * _sigmoid(yn)).astype(BF16)


def _halo_specs(tm, halo, s, width_cols, col_blk=0):
    nh = tm // halo
    last = s // halo - 1
    return [pl.BlockSpec((1, tm, width_cols), lambda bi, i: (bi, i, col_blk)),
            pl.BlockSpec((1, halo, width_cols), lambda bi, i: (bi, jnp.maximum(i * nh - 1, 0), col_blk)),
            pl.BlockSpec((1, halo, width_cols), lambda bi, i: (bi, jnp.minimum((i + 1) * nh, last), col_blk))]


def _lru_kernel(u_ref, up_ref, un_ref, cw_ref, cb_ref, wr_ref, wi_ref, br_ref, bi_ref, lam_ref, h0_ref,
                o_ref, ext_scr, a_scr, b_scr, carry_scr, *, tm, reverse, blocks, halo):
    i, nt = pl.program_id(1), pl.num_programs(1)
    d = u_ref.shape[2]
    bw = d // blocks
    first = (i == 0)
    ti = (nt - 1 - i) if reverse else i

    @pl.when(first)
    def _():
        carry_scr[...] = h0_ref[0]

    ext_scr[0:halo] = jnp.where(ti > 0, up_ref[0], 0.0)
    ext_scr[halo:halo + tm] = u_ref[0]
    ext_scr[halo + tm:] = jnp.where(ti < nt - 1, un_ref[0], 0.0)
    width = cw_ref.shape[0]
    pad_l = width // 2
    u = jnp.broadcast_to(cb_ref[...], (tm, d))
    for k in range(width):
        u = u + cw_ref[k:k + 1, :] * ext_scr[pl.ds(halo - pad_l + k, tm), :]

    ub = u.astype(BF16)
    neg_c_sp = -LRU_C * _softplus(-lam_ref[...])
    for blk in range(blocks):
        sl = slice(blk * bw, (blk + 1) * bw)
        r = _sigmoid(_dot(ub[:, sl], wr_ref[blk]) + br_ref[:, sl])
        g = _sigmoid(_dot(ub[:, sl], wi_ref[blk]) + bi_ref[:, sl])
        log_a = neg_c_sp[:, sl] * r
        a = jnp.exp(log_a)
        a_scr[:, sl] = a
        b_scr[:, sl] = jnp.sqrt(1.0 - a * a) * (g * u[:, sl])

    sub = lax.broadcasted_iota(jnp.int32, (SUBLANE, d), 0)
    ngroups = tm // SUBLANE

    def group(gi, carry):
        r0 = pl.multiple_of(((ngroups - 1 - gi) if reverse else gi) * SUBLANE, SUBLANE)
        a8 = a_scr[pl.ds(r0, SUBLANE), :]
        b8 = b_scr[pl.ds(r0, SUBLANE), :]
        for sft in (1, 2, 4):
            if reverse:
                valid = sub < SUBLANE - sft
                a_sh = pltpu.roll(a8, SUBLANE - sft, 0)
                b_sh = pltpu.roll(b8, SUBLANE - sft, 0)
            else:
                valid = sub >= sft
                a_sh = pltpu.roll(a8, sft, 0)
                b_sh = pltpu.roll(b8, sft, 0)
            b8 = jnp.where(valid, a8 * b_sh + b8, b8)
            a8 = jnp.where(valid, a8 * a_sh, a8)
        h8 = a8 * carry + b8
        o_ref[0, pl.ds(r0, SUBLANE), :] = h8
        return h8[0:1, :] if reverse else h8[SUBLANE - 1:SUBLANE, :]

    carry_scr[...] = lax.fori_loop(0, ngroups, group, carry_scr[...])


def lru_scan(z, h0, conv_w, conv_b, w_r, w_i, b_r, b_i, lam, *, reverse, tm):
    b, s, r2 = z.shape
    r = r2 // 2
    tm = _pick(tm, s)
    nt = s // tm
    halo = SUBLANE
    nh = tm // halo
    last = s // halo - 1

    def ti(i):
        return nt - 1 - i if reverse else i

    vec = lambda: pl.BlockSpec((1, r), lambda bi, i: (0, 0))
    return pl.pallas_call(
        functools.partial(_lru_kernel, tm=tm, reverse=reverse, blocks=w_r.shape[0], halo=halo),
        grid=(b, nt),
        in_specs=[
            pl.BlockSpec((1, tm, r), lambda bi, i: (bi, ti(i), 1)),
            pl.BlockSpec((1, halo, r), lambda bi, i: (bi, jnp.maximum(ti(i) * nh - 1, 0), 1)),
            pl.BlockSpec((1, halo, r), lambda bi, i: (bi, jnp.minimum((ti(i) + 1) * nh, last), 1)),
            pl.BlockSpec(conv_w.shape, lambda bi, i: (0, 0)),
            vec(),
            pl.BlockSpec(w_r.shape, lambda bi, i: (0, 0, 0)),
            pl.BlockSpec(w_i.shape, lambda bi, i: (0, 0, 0)),
            vec(), vec(), vec(),
            pl.BlockSpec((1, 1, r), lambda bi, i: (bi, 0, 0)),
        ],
        out_specs=pl.BlockSpec((1, tm, r), lambda bi, i: (bi, ti(i), 0)),
        out_shape=jax.ShapeDtypeStruct((b, s, r), F32),
        scratch_shapes=[pltpu.VMEM((tm + 2 * halo, r), F32), pltpu.VMEM((tm, r), F32),
                        pltpu.VMEM((tm, r), F32), pltpu.VMEM((1, r), F32)],
        compiler_params=_cp("parallel", "arbitrary"),
        name="lru_scan",
    )(z, z, z, conv_w, conv_b.reshape(1, r), w_r, w_i, b_r.reshape(1, r), b_i.reshape(1, r),
      lam.reshape(1, r), h0.reshape(b, 1, r))


def _lru_out_prologue(pos, r0, rows, hf_ref, hb_ref, gate_ref):
    rs = slice(r0, r0 + rows)
    g = gate_ref[0, rs, :]
    gelu = 0.5 * g * (1.0 + jnp.tanh(0.7978845608028654 * (g + 0.044715 * (g * g * g))))
    return ((hf_ref[0, rs, :] + hb_ref[0, rs, :]) * gelu).astype(BF16)


def _row_spec(tm, cols, col_blk=0):
    return pl.BlockSpec((1, tm, cols), lambda bi, i: (bi, i, col_blk))


def _tiles(s):
    return min(ROW_TILE, s)


def _ptile(s):
    return min(PROJ_TILE, s)


def mlstm_mixer(xl, xc, g, mod_l, mod_c, w_in, w_gate, b_gate, head_g, w_out, need_ctx):
    b, s, d = xl.shape
    heads = ML_HEADS
    nv = w_out.shape[0]
    dv = nv // heads
    nqk = (w_in.shape[1] - 2 * nv) // 2
    dqk = nqk // heads
    n_in = w_in.shape[1]
    ngate = w_gate.shape[1]
    n_all = -(-(n_in + LANE) // MXU_WIDTH) * MXU_WIDTH
    n_pad = n_all - n_in - ngate
    w = jnp.concatenate([w_in, w_gate, jnp.zeros((d, n_pad), F32)], axis=1).astype(BF16)
    bias = jnp.concatenate([jnp.zeros((n_in,), F32), b_gate, jnp.zeros((n_pad,), F32)])
    cos, sin = _rope_tables(s, dqk)
    w_out_b = w_out.astype(BF16)

    def project(x, mod):
        z = norm_proj(x, g, mod, w, bias=bias)
        gates_t = jnp.swapaxes(z[:, :, n_in:n_in + ngate], 1, 2)
        return z, gates_t

    def scan(z, gt, state, reverse, rotary):
        sx = z.shape[1]
        return mlstm_scan(z, gt, cos[:sx], sin[:sx], state, reverse=reverse, rotary=rotary,
                          heads=heads, dqk=dqk, dv=dv, chunk=min(ML_CHUNK, sx))

    def readout(hf, hb, z, x, mod):
        tm = _ptile(x.shape[1])
        o_blk = (2 * nqk + nv) // nv
        specs = [_row_spec(tm, nv), _row_spec(tm, nv), _row_spec(tm, nv, o_blk),
                 pl.BlockSpec((1, nv), lambda bi, i: (0, 0))]
        pro = functools.partial(_mlstm_readout_prologue, heads=heads, dv=dv)
        return resid_proj(pro, [hf, hb, z, head_g.reshape(1, nv)], specs, w_out_b, x, mod, gate_row=2, tm=tm)

    zero = (jnp.zeros((b, heads, dqk, dv), F32), jnp.zeros((b, heads, 1, dqk), F32),
            jnp.zeros((b, heads, 1, LANE), F32))
    zc, gtc = project(xc, mod_c)
    zl, gtl = project(xl, mod_l)
    hc_f, st_f = scan(zc, gtc, zero, False, False)
    hc_b, st_b = scan(zc, gtc, zero, True, False)
    hl_f, _ = scan(zl, gtl, st_f, False, True)
    hl_b, _ = scan(zl, gtl, st_b, True, True)
    yl = readout(hl_f, hl_b, zl, xl, mod_l)
    yc = readout(hc_f, hc_b, zc, xc, mod_c) if need_ctx else None
    return yl, yc


def na_mixer(xl, xc, g, mod_l, mod_c, w_qkv, q_g, k_g, rpb, w_o, need_ctx):
    d = xl.shape[2]
    wb, wob = w_qkv.astype(BF16), w_o.astype(BF16)
    zl = norm_proj(xl, g, mod_l, wb)
    zc = norm_proj(xc, g, mod_c, wb)
    ol = na_attention(zl, zc, q_g, k_g, rpb, heads=NA_HEADS)

    def out(o, x, mod):
        tm = _ptile(x.shape[1])
        return resid_proj(_cast_prologue, [o], [_row_spec(tm, d)], wob, x, mod, gate_row=2, tm=tm)

    yl = out(ol, xl, mod_l)
    yc = out(ctx_attention(zc, q_g, k_g, heads=NA_HEADS), xc, mod_c) if need_ctx else None
    return yl, yc


def conformer_mixer(x, g, mod, w_pw1, dw, dw_b, ln_g, ln_b, w_pw2):
    b, s, d = x.shape
    width = dw.shape[0]
    halo = 16
    assert width // 2 <= halo
    tm = _ptile(s)
    a = norm_proj(x, g, mod, w_pw1.astype(BF16), glu=True)
    vec = lambda: pl.BlockSpec((1, d), lambda bi, i: (0, 0))
    specs = _halo_specs(tm, halo, s, d) + [pl.BlockSpec((width, d), lambda bi, i: (0, 0)),
                                           vec(), vec(), vec()]
    pro = functools.partial(_conformer_prologue, tm=tm, width=width, halo=halo)
    return resid_proj(pro, [a, a, a, dw, dw_b.reshape(1, d), ln_g.reshape(1, d), ln_b.reshape(1, d)],
                      specs, w_pw2.astype(BF16), x, mod, gate_row=2, tm=tm,
                      scratch=[pltpu.VMEM((tm + 2 * halo, d), F32), pltpu.VMEM((tm, d), F32),
                               pltpu.VMEM((4, _conv_rows(tm, width)[1], LANE), F32)])


def rglru_mixer(xl, xc, g, mod_l, mod_c, w_in, conv_w, conv_b, w_gate, b_gate, lam, w_out, need_ctx):
    b, s, d = xl.shape
    r = w_out.shape[0]
    wb, wob, wgb = w_in.astype(BF16), w_out.astype(BF16), w_gate.astype(BF16)
    zl = norm_proj(xl, g, mod_l, wb)
    zc = norm_proj(xc, g, mod_c, wb)

    def scan(z, h0, reverse):
        k = 2 if reverse else 0
        return lru_scan(z, h0, conv_w, conv_b, wgb[k], wgb[k + 1], b_gate[k], b_gate[k + 1],
                        lam[1 if reverse else 0], reverse=reverse, tm=min(LRU_TILE, z.shape[1]))

    h0 = jnp.zeros((b, r), F32)
    hc_f = scan(zc, h0, False)
    hc_b = scan(zc, h0, True)
    hl_f = scan(zl, hc_f[:, -1], False)
    hl_b = scan(zl, hc_b[:, 0], True)

    def out(hf, hb, z, x, mod):
        tm = _ptile(x.shape[1])
        specs = [_row_spec(tm, r), _row_spec(tm, r), _row_spec(tm, r, 0)]
        return resid_proj(_lru_out_prologue, [hf, hb, z], specs, wob, x, mod, gate_row=2, tm=tm)

    yl = out(hl_f, hl_b, zl, xl, mod_l)
    yc = out(hc_f, hc_b, zc, xc, mod_c) if need_ctx else None
    return yl, yc


def kernel(x, c, ctx, c_ctx, norm_mix, norm_ffn, ada_w, ada_b, ml_w_in, ml_w_gate, ml_b_gate, ml_head_g, ml_w_out, na_w_qkv, na_q_g, na_k_g, na_rpb, na_w_o, cv_w_pw1, cv_dw, cv_dw_b, cv_ln_g, cv_ln_b, cv_w_pw2, lr_w_in, lr_conv, lr_conv_b, lr_w_gate, lr_b_gate, lr_lambda, lr_w_out, ffn_w_gu, ffn_conv, ffn_w_down):
    b, s, d = x.shape
    depth = ada_w.shape[0]
    n_mixers = 4
    cc_rows = -(-(b + 1) // SUBLANE) * SUBLANE
    cc = jnp.concatenate([c, c_ctx[None, :], jnp.zeros((cc_rows - b - 1, d), F32)], axis=0)
    mods = ada_modulation(cc, ada_w, ada_b).reshape(depth, cc_rows, 6, d)
    mods = jnp.pad(mods, ((0, 0), (0, 0), (0, MOD_ROWS - 6), (0, 0)))

    xl, xc = x, ctx
    for i in range(depth):
        kind, j = i % n_mixers, i // n_mixers
        need_ctx = i < depth - 1
        mod_l, mod_c = mods[i, :b], mods[i, b:b + 1]
        g = norm_mix[i]
        if kind == 0:
            xl, xc_new = mlstm_mixer(xl, xc, g, mod_l, mod_c, ml_w_in[j], ml_w_gate[j], ml_b_gate[j],
                                     ml_head_g[j], ml_w_out[j], need_ctx)
        elif kind == 1:
            xl, xc_new = na_mixer(xl, xc, g, mod_l, mod_c, na_w_qkv[j], na_q_g[j], na_k_g[j],
                                  na_rpb[j], na_w_o[j], need_ctx)
        elif kind == 2:
            cv = (cv_w_pw1[j], cv_dw[j], cv_dw_b[j], cv_ln_g[j], cv_ln_b[j], cv_w_pw2[j])
            xl, xc_new = conformer_mixer(xl, g, mod_l, *cv), None
            if need_ctx:
                xc_new = conformer_mixer(xc, g, mod_c, *cv)
        else:
            xl, xc_new = rglru_mixer(xl, xc, g, mod_l, mod_c, lr_w_in[j], lr_conv[j], lr_conv_b[j],
                                     lr_w_gate[j], lr_b_gate[j], lr_lambda[j], lr_w_out[j], need_ctx)
        wgu, wdn = ffn_w_gu[i].astype(BF16), ffn_w_down[i].astype(BF16)
        fc = 512 if wdn.shape[0] % 512 == 0 else LANE
        xl = conv_ffn(xl, norm_ffn[i], mod_l, wgu, ffn_conv[i], wdn, tm=ROW_TILE, fc=fc)
        if need_ctx:
            xc = conv_ffn(xc_new, norm_ffn[i], mod_c, wgu, ffn_conv[i], wdn, tm=ROW_TILE, fc=fc)
    return xl
```

```python
import functools

import numpy as np
import jax
import jax.numpy as jnp
from jax import lax
from jax.experimental import pallas as pl
from jax.experimental.pallas import tpu as pltpu

F32 = jnp.float32
BF16 = jnp.bfloat16

EPS = 1e-6
ROPE_THETA = 10000.0
GRID_W = 64
ML_HEADS = 8
ML_CHUNK = 256
ML_BATCH_PER_STEP = 2
NA_HEADS = 16
WIN_ROWS = 8
WIN_COLS = 16
NA_QROWS = 4
NA_HEADS_PER_STEP = 2
NA_QBLOCKS_PER_STEP = 8
LRU_BLOCKS = 8
LRU_C = 8.0
NEG = -1e30

V7X_VMEM_LIMIT = 56 * 1024 * 1024
LANE = 128
SUBLANE = 8
MOD_ROWS = 8
ROW_TILE = 512
PROJ_TILE = 256
PROJ_SUB = 128
MXU_WIDTH = 256
LRU_TILE = 256


def _cp(*sem):
    return pltpu.CompilerParams(dimension_semantics=sem, vmem_limit_bytes=V7X_VMEM_LIMIT)


def _dot(a, b):
    return jnp.dot(a, b, preferred_element_type=F32)


def _dot_nt(a, b):
    return lax.dot_general(a, b, (((1,), (1,)), ((), ())), preferred_element_type=F32)


def _sigmoid(x):
    return 1.0 / (1.0 + jnp.exp(-x))


def _log_sigmoid(x):
    return jnp.minimum(x, 0.0) - jnp.log(1.0 + jnp.exp(-jnp.abs(x)))


def _softplus(x):
    return jnp.maximum(x, 0.0) + jnp.log(1.0 + jnp.exp(-jnp.abs(x)))


def _rms(x, g):
    return x * lax.rsqrt(jnp.mean(x * x, axis=-1, keepdims=True) + EPS) * g


def _norm_mod(x, g, shift, scale):
    return _rms(x, g) * (1.0 + scale) + shift


def _pick(tile, n):
    assert n % tile == 0, (tile, n)
    return tile


def _ada_kernel(c_ref, w_ref, b_ref, o_ref):
    c = c_ref[...]
    s = (c * _sigmoid(c)).astype(BF16)
    o_ref[0] = _dot(s, w_ref[0].astype(BF16)) + b_ref[0]


def ada_modulation(cc, ada_w, ada_b):
    depth, d, n = ada_w.shape
    rows = cc.shape[0]
    tn = _pick(1024 if n % 1024 == 0 else n, n)
    return pl.pallas_call(
        _ada_kernel,
        grid=(depth, n // tn),
        in_specs=[pl.BlockSpec((rows, d), lambda l, j: (0, 0)),
                  pl.BlockSpec((1, d, tn), lambda l, j: (l, 0, j)),
                  pl.BlockSpec((1, 1, tn), lambda l, j: (l, 0, j))],
        out_specs=pl.BlockSpec((1, rows, tn), lambda l, j: (l, 0, j)),
        out_shape=jax.ShapeDtypeStruct((depth, rows, n), F32),
        compiler_params=_cp("parallel", "parallel"),
        name="ada_modulation",
    )(cc, ada_w, ada_b.reshape(depth, 1, n))


def _col_chunk(n):
    for c in (1536, 1280, 1024, 768, 512, 256, 128):
        if n % c == 0:
            return c
    return n


def _resident(shape):
    return pl.BlockSpec(shape, lambda *_: (0,) * len(shape), pipeline_mode=pl.Buffered(1))


def _proj_kernel(x_ref, g_ref, mod_ref, w_ref, *rest, glu, bias, sub):
    if bias:
        b_ref, o_ref = rest
    else:
        (o_ref,) = rest
    tm, n = o_ref.shape[1], o_ref.shape[2]
    nchunk = _col_chunk(n)
    shift = mod_ref[0, 0:1, :]
    gain = g_ref[...] * (1.0 + mod_ref[0, 1:2, :])
    for r0 in range(0, tm, sub):
        x = x_ref[0, r0:r0 + sub, :]
        h = (x * lax.rsqrt(jnp.mean(x * x, axis=-1, keepdims=True) + EPS) * gain + shift).astype(BF16)
        for c0 in range(0, n, nchunk):
            y = _dot(h, w_ref[:, c0:c0 + nchunk])
            if glu:
                y = y * _sigmoid(_dot(h, w_ref[:, n + c0:n + c0 + nchunk]))
            if bias:
                y = y + b_ref[:, c0:c0 + nchunk]
            o_ref[0, r0:r0 + sub, c0:c0 + nchunk] = y


def norm_proj(x, g, mod, w, *, bias=None, glu=False):
    b, s, d = x.shape
    n = w.shape[1] // 2 if glu else w.shape[1]
    tm = min(PROJ_TILE, s)
    sub = min(PROJ_SUB, tm)
    assert s % tm == 0 and tm % sub == 0
    per_batch = mod.shape[0] != 1
    in_specs = [pl.BlockSpec((1, tm, d), lambda bi, i: (bi, i, 0)),
                pl.BlockSpec((1, d), lambda bi, i: (0, 0)),
                pl.BlockSpec((1, MOD_ROWS, d), lambda bi, i: (bi if per_batch else 0, 0, 0)),
                _resident(w.shape)]
    args = [x, g.reshape(1, d), mod, w]
    if bias is not None:
        in_specs.append(pl.BlockSpec((1, n), lambda bi, i: (0, 0)))
        args.append(bias.reshape(1, n))
    return pl.pallas_call(
        functools.partial(_proj_kernel, glu=glu, bias=bias is not None, sub=sub),
        grid=(b, s // tm),
        in_specs=in_specs,
        out_specs=pl.BlockSpec((1, tm, n), lambda bi, i: (bi, i, 0)),
        out_shape=jax.ShapeDtypeStruct((b, s, n), F32),
        compiler_params=_cp("parallel", "parallel"),
        name="norm_proj",
    )(*args)


def _resid_kernel(*refs, prologue, n_in, gate_row, sub):
    in_refs = refs[:n_in]
    w_ref, x_ref, mod_ref, o_ref, *scr = refs[n_in:]
    pos = (pl.program_id(1), pl.num_programs(1))
    tm, n = x_ref.shape[1], x_ref.shape[2]
    nchunk = _col_chunk(n)
    gate = mod_ref[0, gate_row:gate_row + 1, :]
    for r0 in range(0, tm, sub):
        h = prologue(pos, r0, sub, *in_refs, *scr)
        for c0 in range(0, n, nchunk):
            y = _dot(h, w_ref[:, c0:c0 + nchunk])
            o_ref[0, r0:r0 + sub, c0:c0 + nchunk] = (x_ref[0, r0:r0 + sub, c0:c0 + nchunk]
                                                    + gate[:, c0:c0 + nchunk] * y)


def resid_proj(prologue, ins, in_specs, w, x, mod, *, gate_row, tm, scratch=()):
    b, s, n = x.shape
    sub = min(PROJ_SUB, tm)
    assert s % tm == 0 and tm % sub == 0
    per_batch = mod.shape[0] != 1
    specs = list(in_specs) + [
        _resident(w.shape),
        pl.BlockSpec((1, tm, n), lambda bi, i: (bi, i, 0)),
        pl.BlockSpec((1, MOD_ROWS, n), lambda bi, i: (bi if per_batch else 0, 0, 0))]
    return pl.pallas_call(
        functools.partial(_resid_kernel, prologue=prologue, n_in=len(ins), gate_row=gate_row, sub=sub),
        grid=(b, s // tm),
        in_specs=specs,
        out_specs=pl.BlockSpec((1, tm, n), lambda bi, i: (bi, i, 0)),
        out_shape=jax.ShapeDtypeStruct((b, s, n), F32),
        scratch_shapes=list(scratch),
        compiler_params=_cp("parallel", "parallel"),
        name="resid_proj",
    )(*ins, w, x, mod)


def _ffn_kernel(x_ref, xp_ref, xn_ref, g_ref, mod_ref, wg_ref, wu_ref, cw_ref, wd_ref, o_ref,
                h_scr, g_scr, acc_scr, *, tm, seq):
    i, j = pl.program_id(1), pl.program_id(2)
    nt, nf = pl.num_programs(1), pl.num_programs(2)
    hb = SUBLANE

    @pl.when(j == 0)
    def _():
        g, shift, scale = g_ref[...], mod_ref[0, 3:4, :], mod_ref[0, 4:5, :]
        h_scr[0:tm] = _norm_mod(x_ref[0], g, shift, scale).astype(BF16)
        halo = jnp.concatenate([xp_ref[0], xn_ref[0]], axis=0)
        h_scr[tm:] = _norm_mod(halo, g, shift, scale).astype(BF16)
        acc_scr[...] = jnp.zeros_like(acc_scr)

    ga = _dot(h_scr[...], wg_ref[...])
    uu = _dot(h_scr[0:tm], wu_ref[...])
    gg = ga[0:tm]
    g_scr[0:hb] = jnp.where(i > 0, ga[tm:tm + hb], 0.0)
    g_scr[hb:hb + tm] = gg
    g_scr[hb + tm:] = jnp.where(i < nt - 1, ga[tm + hb:], 0.0)
    cw = cw_ref[...]
    g_prev, g_next = g_scr[pl.ds(hb - 1, tm), :], g_scr[pl.ds(hb + 1, tm), :]
    if seq < tm:
        pos = lax.broadcasted_iota(jnp.int32, (tm, 1), 0) % seq
        g_prev = jnp.where(pos > 0, g_prev, 0.0)
        g_next = jnp.where(pos < seq - 1, g_next, 0.0)
    gc = cw[0:1] * g_prev + cw[1:2] * gg + cw[2:3] * g_next
    act = (gc * _sigmoid(gc) * uu).astype(BF16)
    acc_scr[...] += _dot(act, wd_ref[...])

    @pl.when(j == nf - 1)
    def _():
        o_ref[0] = x_ref[0] + mod_ref[0, 5:6, :] * acc_scr[...]


def conv_ffn(x, g, mod, w_gu, conv_w, w_down, *, tm, fc):
    seq = x.shape[1]
    per_batch = mod.shape[0] != 1
    if not per_batch and tm % seq == 0:
        out = _conv_ffn(x.reshape(1, -1, x.shape[2]), g, mod, w_gu, conv_w, w_down, tm=tm, fc=fc, seq=seq)
        return out.reshape(x.shape)
    return _conv_ffn(x, g, mod, w_gu, conv_w, w_down, tm=tm, fc=fc, seq=seq)


def _conv_ffn(x, g, mod, w_gu, conv_w, w_down, *, tm, fc, seq):
    b, s, d = x.shape
    f = w_down.shape[0]
    tm, fc = _pick(min(tm, s), s), _pick(fc, f)
    assert seq == s or tm % seq == 0
    per_batch = mod.shape[0] != 1
    nh = tm // SUBLANE
    last_h = s // SUBLANE - 1
    return pl.pallas_call(
        functools.partial(_ffn_kernel, tm=tm, seq=seq),
        grid=(b, s // tm, f // fc),
        in_specs=[
            pl.BlockSpec((1, tm, d), lambda bi, i, j: (bi, i, 0)),
            pl.BlockSpec((1, SUBLANE, d), lambda bi, i, j: (bi, jnp.maximum(i * nh - 1, 0), 0)),
            pl.BlockSpec((1, SUBLANE, d), lambda bi, i, j: (bi, jnp.minimum((i + 1) * nh, last_h), 0)),
            pl.BlockSpec((1, d), lambda bi, i, j: (0, 0)),
            pl.BlockSpec((1, MOD_ROWS, d), lambda bi, i, j: (bi if per_batch else 0, 0, 0)),
            pl.BlockSpec((d, fc), lambda bi, i, j: (0, j)),
            pl.BlockSpec((d, fc), lambda bi, i, j: (0, j + f // fc)),
            pl.BlockSpec((conv_w.shape[0], fc), lambda bi, i, j: (0, j)),
            pl.BlockSpec((fc, d), lambda bi, i, j: (j, 0)),
        ],
        out_specs=pl.BlockSpec((1, tm, d), lambda bi, i, j: (bi, i, 0)),
        out_shape=jax.ShapeDtypeStruct((b, s, d), F32),
        scratch_shapes=[pltpu.VMEM((tm + 2 * SUBLANE, d), BF16),
                        pltpu.VMEM((tm + 2 * SUBLANE, fc), F32),
                        pltpu.VMEM((tm, d), F32)],
        compiler_params=_cp("parallel", "parallel", "arbitrary"),
        name="conv_ffn",
    )(x, x, x, g.reshape(1, d), mod, w_gu, w_gu, conv_w, w_down)


def _split3(x):
    hi = x.astype(BF16)
    r = x - hi.astype(F32)
    mid = r.astype(BF16)
    lo = (r - mid.astype(F32)).astype(BF16)
    return hi, mid, lo


def _rope_perm(d):
    half = d // 4
    src = lax.broadcasted_iota(jnp.int32, (d, d), 0)
    dst = lax.broadcasted_iota(jnp.int32, (d, d), 1)
    partner = jnp.where((dst % (2 * half)) < half, dst + half, dst - half)
    return jnp.where(src == partner, 1.0, 0.0).astype(BF16)


def _rope(x, cos, sin_signed, perm):
    return x * cos + _dot(x.astype(BF16), perm) * sin_signed


def _mlstm_kernel(q_ref, k_ref, v_ref, gc_ref, gr_ref, cos_ref, sin_ref, c0_ref, n0_ref, m0_ref,
                  h_ref, c_out, n_out, m_out, c_scr, n_scr, m_scr, *, reverse, rotary, heads, dqk, dv):
    c = pl.program_id(1)
    nc = pl.num_programs(1)
    nb, L = q_ref.shape[0], q_ref.shape[1]

    @pl.when(c == 0)
    def _():
        c_scr[...] = c0_ref[...]
        n_scr[...] = n0_ref[...]
        m_scr[...] = m0_ref[...]

    off_i = 2 * heads if reverse else 0
    off_f = off_i + heads
    lane = lax.broadcasted_iota(jnp.int32, (L, gc_ref.shape[2]), 1)

    row = lax.broadcasted_iota(jnp.int32, (L, L), 0)
    col = lax.broadcasted_iota(jnp.int32, (L, L), 1)
    mask = (col >= row) if reverse else (col <= row)
    mask_t = (row >= col) if reverse else (row <= col)
    tri = jnp.where(mask, 1.0, 0.0).astype(BF16)
    tri_t = jnp.where(mask_t, 1.0, 0.0).astype(BF16)
    perm = _rope_perm(dqk) if rotary else None

    for bb in range(nb):
        gcol = gc_ref[bb]
        grow = gr_ref[bb]
        lf_rows = _log_sigmoid(grow)
        b_all = sum(_dot(tri, p) for p in _split3(_log_sigmoid(gcol)))
        b_rows = sum(_dot(p, tri_t) for p in _split3(lf_rows))
        for hd in range(heads):
            qs, vs = slice(hd * dqk, (hd + 1) * dqk), slice(hd * dv, (hd + 1) * dv)
            q = q_ref[bb, :, qs] * (dqk ** -0.5)
            k = k_ref[bb, :, qs]
            if rotary:
                q = _rope(q, cos_ref[...], sin_ref[...], perm)
                k = _rope(k, cos_ref[...], sin_ref[...], perm)
            qb, kb, vb = q.astype(BF16), k.astype(BF16), v_ref[bb, :, vs].astype(BF16)

            li_col = jnp.sum(jnp.where(lane == off_i + hd, gcol, 0.0), axis=-1, keepdims=True)
            b_col = jnp.sum(jnp.where(lane == off_f + hd, b_all, 0.0), axis=-1, keepdims=True)
            li_row = grow[off_i + hd:off_i + hd + 1]
            b_row = b_rows[off_f + hd:off_f + hd + 1]
            total = jnp.sum(lf_rows[off_f + hd:off_f + hd + 1], axis=-1, keepdims=True)

            m_prev = m_scr[bb, hd, 0:1, 0:1]
            rmat = jnp.where(mask, li_row - b_row, NEG)
            g_col = jnp.maximum(m_prev, jnp.max(rmat, axis=-1, keepdims=True))
            m_t = b_col + g_col
            w_inter = jnp.exp(m_prev - g_col)
            s_qk = _dot_nt(qb, kb) * jnp.exp(rmat - g_col)
            num = w_inter * _dot(qb, c_scr[bb, hd].astype(BF16)) + _dot(s_qk.astype(BF16), vb)
            den = (w_inter * jnp.sum(q * n_scr[bb, hd], axis=-1, keepdims=True)
                   + jnp.sum(s_qk, axis=-1, keepdims=True))
            h_ref[bb, :, vs] = num / jnp.maximum(jnp.abs(den), jnp.exp(-m_t))

            w_col = total - b_col + li_col
            m_new = jnp.maximum(total + m_prev, jnp.max(w_col, axis=0, keepdims=True))
            decay = jnp.exp(total + m_prev - m_new)
            ek = k * jnp.exp(w_col - m_new)
            c_scr[bb, hd] = decay * c_scr[bb, hd] + _dot(ek.T.astype(BF16), vb)
            n_scr[bb, hd] = decay * n_scr[bb, hd] + jnp.sum(ek, axis=0, keepdims=True)
            m_scr[bb, hd] = jnp.broadcast_to(m_new, (1, m_scr.shape[3]))

    @pl.when(c == nc - 1)
    def _():
        c_out[...] = c_scr[...]
        n_out[...] = n_scr[...]
        m_out[...] = m_scr[...]


def mlstm_scan(z, gates_t, cos, sin, state, *, reverse, rotary, heads, dqk, dv, chunk):
    b, s, _ = z.shape
    L = _pick(chunk, s)
    nc = s // L
    nqk, nv = heads * dqk, heads * dv
    assert (2 * nqk) % nv == 0
    gate_blk = (2 * nqk + 2 * nv) // LANE

    def ci(c):
        return nc - 1 - c if reverse else c

    c0, n0, m0 = state
    nb = ML_BATCH_PER_STEP if b % ML_BATCH_PER_STEP == 0 else 1
    st_specs = [pl.BlockSpec((nb, heads, dqk, dv), lambda bi, c: (bi, 0, 0, 0)),
                pl.BlockSpec((nb, heads, 1, dqk), lambda bi, c: (bi, 0, 0, 0)),
                pl.BlockSpec((nb, heads, 1, LANE), lambda bi, c: (bi, 0, 0, 0))]
    h, c1, n1, m1 = pl.pallas_call(
        functools.partial(_mlstm_kernel, reverse=reverse, rotary=rotary, heads=heads, dqk=dqk, dv=dv),
        grid=(b // nb, nc),
        in_specs=[
            pl.BlockSpec((nb, L, nqk), lambda bi, c: (bi, ci(c), 0)),
            pl.BlockSpec((nb, L, nqk), lambda bi, c: (bi, ci(c), 1)),
            pl.BlockSpec((nb, L, nv), lambda bi, c: (bi, ci(c), 2 * nqk // nv)),
            pl.BlockSpec((nb, L, LANE), lambda bi, c: (bi, ci(c), gate_blk)),
            pl.BlockSpec((nb, 4 * heads, L), lambda bi, c: (bi, 0, ci(c))),
            pl.BlockSpec((L, dqk), lambda bi, c: (ci(c), 0)),
            pl.BlockSpec((L, dqk), lambda bi, c: (ci(c), 0)),
        ] + st_specs,
        out_specs=[pl.BlockSpec((nb, L, nv), lambda bi, c: (bi, ci(c), 0))] + st_specs,
        out_shape=[jax.ShapeDtypeStruct((b, s, nv), F32),
                   jax.ShapeDtypeStruct(c0.shape, F32),
                   jax.ShapeDtypeStruct(n0.shape, F32),
                   jax.ShapeDtypeStruct(m0.shape, F32)],
        scratch_shapes=[pltpu.VMEM((nb, heads, dqk, dv), F32), pltpu.VMEM((nb, heads, 1, dqk), F32),
                        pltpu.VMEM((nb, heads, 1, LANE), F32)],
        compiler_params=_cp("parallel", "arbitrary"),
        name="mlstm_scan",
    )(z, z, z, z, gates_t, cos, sin, c0, n0, m0)
    return h, (c1, n1, m1)


def _rope_tables(s, d):
    half, quarter = d // 2, d // 4
    freqs = ROPE_THETA ** (-jnp.arange(0, half, 2, dtype=F32) / half)
    t = jnp.arange(s)
    ang_r = (t // GRID_W).astype(F32)[:, None] * freqs[None, :]
    ang_c = (t % GRID_W).astype(F32)[:, None] * freqs[None, :]
    cos = jnp.concatenate([jnp.cos(ang_r)] * 2 + [jnp.cos(ang_c)] * 2, axis=-1)
    sin = jnp.concatenate([-jnp.sin(ang_r), jnp.sin(ang_r), -jnp.sin(ang_c), jnp.sin(ang_c)], axis=-1)
    del quarter
    return cos, sin


def _mlstm_readout_prologue(pos, r0, rows, hf_ref, hb_ref, o_ref, g_ref, *, heads, dv):
    rs = slice(r0, r0 + rows)
    parts = []
    for hd in range(heads):
        sl = slice(hd * dv, (hd + 1) * dv)
        hs = hf_ref[0, rs, sl] + hb_ref[0, rs, sl]
        parts.append((_rms(hs, g_ref[:, sl]) * _sigmoid(o_ref[0, rs, sl])).astype(BF16))
    return jnp.concatenate(parts, axis=-1)


def _na_kernel(q_ref, k_ref, v_ref, kc_ref, vc_ref, qg_ref, kg_ref, bias_ref, o_ref,
               kn_scr, vn_scr, kcn_scr, vcn_scr, *, nkeys, kstep, half, kr_max, hps, dh, rps):
    nb = pl.num_programs(2) * rps
    tq = q_ref.shape[1] // rps

    @pl.when(pl.program_id(2) == 0)
    def _():
        for hh in range(hps):
            sl = slice(hh * dh, (hh + 1) * dh)
            kn_scr[hh] = _rms(k_ref[0, :, sl], kg_ref[...]).astype(BF16)
            kcn_scr[hh] = _rms(kc_ref[0, :, sl], kg_ref[...]).astype(BF16)
            vn_scr[hh, :, 0:dh] = v_ref[0, :, sl].astype(BF16)
            vn_scr[hh, :, dh:] = jnp.ones((vn_scr.shape[1], dh), BF16)
            vcn_scr[hh, :, 0:dh] = vc_ref[0, :, sl].astype(BF16)
            vcn_scr[hh, :, dh:] = jnp.ones((vcn_scr.shape[1], dh), BF16)

    for rr in range(rps):
        rb = pl.program_id(2) * rps + rr
        rows = slice(rr * tq, (rr + 1) * tq)
        kr0 = jnp.clip(rb * (tq // kstep) - half, 0, kr_max)
        start = pl.multiple_of(kr0 * kstep, kstep)
        cls = jnp.where(rb == 0, 0, jnp.where(rb == nb - 1, 2, 1))
        for hh in range(hps):
            sl = slice(hh * dh, (hh + 1) * dh)
            qn = (_rms(q_ref[0, rows, sl], qg_ref[...]) * (dh ** -0.5)).astype(BF16)
            s_loc = _dot_nt(qn, kn_scr[hh, pl.ds(start, nkeys), :]) + bias_ref[hh, cls]
            s_ctx = _dot_nt(qn, kcn_scr[hh])
            m = jnp.maximum(jnp.max(s_loc, axis=-1, keepdims=True), jnp.max(s_ctx, axis=-1, keepdims=True))
            p_loc = jnp.exp(s_loc - m).astype(BF16)
            p_ctx = jnp.exp(s_ctx - m).astype(BF16)
            ol = _dot(p_loc, vn_scr[hh, pl.ds(start, nkeys), :]) + _dot(p_ctx, vcn_scr[hh])
            o_ref[0, rows, sl] = ol[:, 0:dh] / ol[:, dh:dh + 1]


def _na_bias_tables(rows):
    W, R, half = GRID_W, NA_QROWS, WIN_ROWS // 2
    win_r = min(WIN_ROWS, rows)
    U = R + win_r - 1
    nb = rows // R
    kr_max = rows - U
    assert nb >= 3 and R >= half and kr_max >= 0
    ri, kj = np.arange(R)[:, None], np.arange(U)[None, :]
    per_block = []
    for rb in range(nb):
        r = rb * R + ri
        kr = np.clip(rb * R - half, 0, kr_max) + kj
        rs = np.clip(r - half, 0, rows - win_r)
        ok = (kr >= rs) & (kr < rs + win_r)
        dr = np.clip(kr - r + WIN_ROWS - 1, 0, 2 * WIN_ROWS - 2)
        per_block.append((ok, dr))
    for rb in range(2, nb - 1):
        assert all(np.array_equal(a, b_) for a, b_ in zip(per_block[1], per_block[rb]))
    return [per_block[0], per_block[1], per_block[nb - 1]], U, kr_max


def _na_bias_kernel(rpb_ref, o_ref, t_scr, *, classes):
    h = pl.program_id(0)
    W = t_scr.shape[1]
    ndr, ndc = 2 * WIN_ROWS - 1, 2 * WIN_COLS - 1
    c = lax.broadcasted_iota(jnp.int32, (W, W), 0)
    kc = lax.broadcasted_iota(jnp.int32, (W, W), 1)
    cstart = jnp.clip(c - WIN_COLS // 2, 0, W - WIN_COLS)
    dcidx = jnp.clip(kc - c + WIN_COLS - 1, 0, ndc - 1)
    for dr in range(ndr):
        t = jnp.full((W, W), NEG, F32)
        for dc in range(ndc):
            t = jnp.where(dcidx == dc, rpb_ref[h, dr * ndc + dc], t)
        t = jnp.where(kc >= cstart, t, NEG)
        t_scr[dr] = jnp.where(kc < cstart + WIN_COLS, t, NEG)
    o_ref[...] = jnp.full(o_ref.shape, NEG, F32)
    for cls, (ok, dr) in enumerate(classes):
        for ri in range(ok.shape[0]):
            for kj in range(ok.shape[1]):
                if ok[ri, kj]:
                    o_ref[0, cls, ri * W:(ri + 1) * W, kj * W:(kj + 1) * W] = t_scr[int(dr[ri, kj])]


def na_bias_table(rpb, classes, tq, nkeys):
    heads = rpb.shape[0]
    return pl.pallas_call(
        functools.partial(_na_bias_kernel, classes=classes),
        grid=(heads,),
        in_specs=[pl.BlockSpec(memory_space=pltpu.SMEM)],
        out_specs=pl.BlockSpec((1, len(classes), tq, nkeys), lambda h: (h, 0, 0, 0)),
        out_shape=jax.ShapeDtypeStruct((heads, len(classes), tq, nkeys), F32),
        scratch_shapes=[pltpu.VMEM((2 * WIN_ROWS - 1, GRID_W, GRID_W), F32)],
        compiler_params=_cp("parallel"),
        name="na_bias_table",
    )(rpb.reshape(heads, -1))


def na_attention(z, zc, q_g, k_g, rpb, *, heads):
    b, s, n3 = z.shape
    dh = n3 // (3 * heads)
    cl = zc.shape[1]
    rows = s // GRID_W
    classes, U, kr_max = _na_bias_tables(rows)
    tq, nkeys = NA_QROWS * GRID_W, U * GRID_W
    bias = na_bias_table(rpb, classes, tq, nkeys)
    hps, rps = NA_HEADS_PER_STEP, NA_QBLOCKS_PER_STEP
    hg, w = heads // hps, hps * dh
    assert (s // tq) % rps == 0
    return pl.pallas_call(
        functools.partial(_na_kernel, nkeys=nkeys, kstep=GRID_W, half=WIN_ROWS // 2, kr_max=kr_max,
                          hps=hps, dh=dh, rps=rps),
        grid=(b, hg, s // (tq * rps)),
        in_specs=[
            pl.BlockSpec((1, tq * rps, w), lambda bi, h, r: (bi, r, h)),
            pl.BlockSpec((1, s, w), lambda bi, h, r: (bi, 0, hg + h)),
            pl.BlockSpec((1, s, w), lambda bi, h, r: (bi, 0, 2 * hg + h)),
            pl.BlockSpec((1, cl, w), lambda bi, h, r: (bi, 0, hg + h)),
            pl.BlockSpec((1, cl, w), lambda bi, h, r: (bi, 0, 2 * hg + h)),
            pl.BlockSpec((1, dh), lambda bi, h, r: (0, 0)),
            pl.BlockSpec((1, dh), lambda bi, h, r: (0, 0)),
            pl.BlockSpec((hps, 3, tq, nkeys), lambda bi, h, r: (h, 0, 0, 0)),
        ],
        out_specs=pl.BlockSpec((1, tq * rps, w), lambda bi, h, r: (bi, r, h)),
        out_shape=jax.ShapeDtypeStruct((b, s, heads * dh), F32),
        scratch_shapes=[pltpu.VMEM((hps, s, dh), BF16), pltpu.VMEM((hps, s, 2 * dh), BF16),
                        pltpu.VMEM((hps, cl, dh), BF16), pltpu.VMEM((hps, cl, 2 * dh), BF16)],
        compiler_params=_cp("parallel", "parallel", "arbitrary"),
        name="na_attention",
    )(z, z, z, zc, zc, q_g.reshape(1, dh), k_g.reshape(1, dh), bias)


def _ctx_attn_kernel(q_ref, k_ref, v_ref, qg_ref, kg_ref, o_ref):
    dh = q_ref.shape[2]
    qn = (_rms(q_ref[0], qg_ref[...]) * (dh ** -0.5)).astype(BF16)
    kn = _rms(k_ref[0], kg_ref[...]).astype(BF16)
    s = _dot_nt(qn, kn)
    p = jnp.exp(s - jnp.max(s, axis=-1, keepdims=True))
    l = jnp.sum(p, axis=-1, keepdims=True)
    o_ref[0] = _dot(p.astype(BF16), v_ref[0].astype(BF16)) / l


def ctx_attention(zc, q_g, k_g, *, heads):
    b, cl, n3 = zc.shape
    dh = n3 // (3 * heads)
    return pl.pallas_call(
        _ctx_attn_kernel,
        grid=(b, heads),
        in_specs=[pl.BlockSpec((1, cl, dh), lambda bi, h: (bi, 0, h)),
                  pl.BlockSpec((1, cl, dh), lambda bi, h: (bi, 0, heads + h)),
                  pl.BlockSpec((1, cl, dh), lambda bi, h: (bi, 0, 2 * heads + h)),
                  pl.BlockSpec((1, dh), lambda bi, h: (0, 0)),
                  pl.BlockSpec((1, dh), lambda bi, h: (0, 0))],
        out_specs=pl.BlockSpec((1, cl, dh), lambda bi, h: (bi, 0, h)),
        out_shape=jax.ShapeDtypeStruct((b, cl, heads * dh), F32),
        compiler_params=_cp("parallel", "parallel"),
        name="ctx_attention",
    )(zc, zc, zc, q_g.reshape(1, dh), k_g.reshape(1, dh))


def _cast_prologue(pos, r0, rows, a_ref):
    return a_ref[0, r0:r0 + rows, :].astype(BF16)


def _conv_rows(tm, width):
    rsub = min(tm, PROJ_SUB)
    return rsub, rsub + (-(-width // SUBLANE) - 1) * SUBLANE


def _conformer_prologue(pos, r0, rows, a_ref, ap_ref, an_ref, dw_ref, dwb_ref, lg_ref, lb_ref, ext_scr, y_scr,
                        sh_scr, *, tm, width, halo):
    i, nt = pos
    d = a_ref.shape[2]
    pad = width // 2
    if r0 == 0:
        ext_scr[0:halo] = jnp.where(i > 0, ap_ref[0], 0.0)
        ext_scr[halo:halo + tm] = a_ref[0]
        ext_scr[halo + tm:] = jnp.where(i < nt - 1, an_ref[0], 0.0)

    rsub, span = _conv_rows(tm, width)
    assert rows == rsub
    n_a = -(-width // SUBLANE)

    def col_block(cb, carry):
        cs = pl.ds(pl.multiple_of(cb * LANE, LANE), LANE)
        acc = jnp.broadcast_to(dwb_ref[:, cs], (rsub, LANE))
        for b in range(SUBLANE):
            slot = b % sh_scr.shape[0]
            sh_scr[slot] = ext_scr[pl.ds(halo - pad + r0 + b, span), cs]
            for a in range(n_a):
                k = a * SUBLANE + b
                if k < width:
                    acc = acc + dw_ref[k:k + 1, cs] * sh_scr[slot, a * SUBLANE:a * SUBLANE + rsub, :]
        y_scr[r0:r0 + rsub, cs] = acc
        return carry

    lax.fori_loop(0, d // LANE, col_block, 0)
    y = y_scr[r0:r0 + rsub, :]
    mu = jnp.mean(y, axis=-1, keepdims=True)
    var = jnp.mean(jnp.square(y - mu), axis=-1, keepdims=True)
    yn = (y - mu) * lax.rsqrt(var + EPS) * lg_ref[...] + lb_ref[...]
    return (yn * _sigmoid(yn)).astype(BF16)


def _halo_specs(tm, halo, s, width_cols, col_blk=0):
    nh = tm // halo
    last = s // halo - 1
    return [pl.BlockSpec((1, tm, width_cols), lambda bi, i: (bi, i, col_blk)),
            pl.BlockSpec((1, halo, width_cols), lambda bi, i: (bi, jnp.maximum(i * nh - 1, 0), col_blk)),
            pl.BlockSpec((1, halo, width_cols), lambda bi, i: (bi, jnp.minimum((i + 1) * nh, last), col_blk))]


def _lru_kernel(u_ref, up_ref, un_ref, cw_ref, cb_ref, wr_ref, wi_ref, br_ref, bi_ref, lam_ref, h0_ref,
                o_ref, ext_scr, a_scr, b_scr, carry_scr, *, tm, reverse, blocks, halo):
    i, nt = pl.program_id(1), pl.num_programs(1)
    d = u_ref.shape[2]
    bw = d // blocks
    first = (i == 0)
    ti = (nt - 1 - i) if reverse else i

    @pl.when(first)
    def _():
        carry_scr[...] = h0_ref[0]

    ext_scr[0:halo] = jnp.where(ti > 0, up_ref[0], 0.0)
    ext_scr[halo:halo + tm] = u_ref[0]
    ext_scr[halo + tm:] = jnp.where(ti < nt - 1, un_ref[0], 0.0)
    width = cw_ref.shape[0]
    pad_l = width // 2
    u = jnp.broadcast_to(cb_ref[...], (tm, d))
    ext = ext_scr[...]
    for k in range(width):
        shifted = ext if k == pad_l else pltpu.roll(ext, (pad_l - k) % ext.shape[0], 0)
        u = u + cw_ref[k:k + 1, :] * shifted[halo:halo + tm]

    ub = u.astype(BF16)
    neg_c_sp = -LRU_C * _softplus(-lam_ref[...])
    for blk in range(blocks):
        sl = slice(blk * bw, (blk + 1) * bw)
        r = _sigmoid(_dot(ub[:, sl], wr_ref[blk]) + br_ref[:, sl])
        g = _sigmoid(_dot(ub[:, sl], wi_ref[blk]) + bi_ref[:, sl])
        log_a = neg_c_sp[:, sl] * r
        a = jnp.exp(log_a)
        a_scr[:, sl] = a
        b_scr[:, sl] = jnp.sqrt(1.0 - a * a) * (g * u[:, sl])

    sub = lax.broadcasted_iota(jnp.int32, (SUBLANE, d), 0)
    ngroups = tm // SUBLANE

    def group(gi, carry):
        r0 = pl.multiple_of(((ngroups - 1 - gi) if reverse else gi) * SUBLANE, SUBLANE)
        a8 = a_scr[pl.ds(r0, SUBLANE), :]
        b8 = b_scr[pl.ds(r0, SUBLANE), :]
        for sft in (1, 2, 4):
            if reverse:
                valid = sub < SUBLANE - sft
                a_sh = pltpu.roll(a8, SUBLANE - sft, 0)
                b_sh = pltpu.roll(b8, SUBLANE - sft, 0)
            else:
                valid = sub >= sft
                a_sh = pltpu.roll(a8, sft, 0)
                b_sh = pltpu.roll(b8, sft, 0)
            b8 = jnp.where(valid, a8 * b_sh + b8, b8)
            a8 = jnp.where(valid, a8 * a_sh, a8)
        h8 = a8 * carry + b8
        o_ref[0, pl.ds(r0, SUBLANE), :] = h8
        return h8[0:1, :] if reverse else h8[SUBLANE - 1:SUBLANE, :]

    carry_scr[...] = lax.fori_loop(0, ngroups, group, carry_scr[...])


def lru_scan(z, h0, conv_w, conv_b, w_r, w_i, b_r, b_i, lam, *, reverse, tm):
    b, s, r2 = z.shape
    r = r2 // 2
    tm = _pick(tm, s)
    nt = s // tm
    halo = SUBLANE
    nh = tm // halo
    last = s // halo - 1

    def ti(i):
        return nt - 1 - i if reverse else i

    vec = lambda: pl.BlockSpec((1, r), lambda bi, i: (0, 0))
    return pl.pallas_call(
        functools.partial(_lru_kernel, tm=tm, reverse=reverse, blocks=w_r.shape[0], halo=halo),
        grid=(b, nt),
        in_specs=[
            pl.BlockSpec((1, tm, r), lambda bi, i: (bi, ti(i), 1)),
            pl.BlockSpec((1, halo, r), lambda bi, i: (bi, jnp.maximum(ti(i) * nh - 1, 0), 1)),
            pl.BlockSpec((1, halo, r), lambda bi, i: (bi, jnp.minimum((ti(i) + 1) * nh, last), 1)),
            pl.BlockSpec(conv_w.shape, lambda bi, i: (0, 0)),
            vec(),
            pl.BlockSpec(w_r.shape, lambda bi, i: (0, 0, 0)),
            pl.BlockSpec(w_i.shape, lambda bi, i: (0, 0, 0)),
            vec(), vec(), vec(),
            pl.BlockSpec((1, 1, r), lambda bi, i: (bi, 0, 0)),
        ],
        out_specs=pl.BlockSpec((1, tm, r), lambda bi, i: (bi, ti(i), 0)),
        out_shape=jax.ShapeDtypeStruct((b, s, r), F32),
        scratch_shapes=[pltpu.VMEM((tm + 2 * halo, r), F32), pltpu.VMEM((tm, r), F32),
                        pltpu.VMEM((tm, r), F32), pltpu.VMEM((1, r), F32)],
        compiler_params=_cp("parallel", "arbitrary"),
        name="lru_scan",
    )(z, z, z, conv_w, conv_b.reshape(1, r), w_r, w_i, b_r.reshape(1, r), b_i.reshape(1, r),
      lam.reshape(1, r), h0.reshape(b, 1, r))


def _lru_out_prologue(pos, r0, rows, hf_ref, hb_ref, gate_ref):
    rs = slice(r0, r0 + rows)
    g = gate_ref[0, rs, :]
    gelu = 0.5 * g * (1.0 + jnp.tanh(0.7978845608028654 * (g + 0.044715 * (g * g * g))))
    return ((hf_ref[0, rs, :] + hb_ref[0, rs, :]) * gelu).astype(BF16)


def _row_spec(tm, cols, col_blk=0):
    return pl.BlockSpec((1, tm, cols), lambda bi, i: (bi, i, col_blk))


def _tiles(s):
    return min(ROW_TILE, s)


def _ptile(s):
    return min(PROJ_TILE, s)


def mlstm_mixer(xl, xc, g, mod_l, mod_c, w_in, w_gate, b_gate, head_g, w_out, need_ctx):
    b, s, d = xl.shape
    heads = ML_HEADS
    nv = w_out.shape[0]
    dv = nv // heads
    nqk = (w_in.shape[1] - 2 * nv) // 2
    dqk = nqk // heads
    n_in = w_in.shape[1]
    ngate = w_gate.shape[1]
    n_all = -(-(n_in + LANE) // MXU_WIDTH) * MXU_WIDTH
    n_pad = n_all - n_in - ngate
    w = jnp.concatenate([w_in, w_gate, jnp.zeros((d, n_pad), F32)], axis=1).astype(BF16)
    bias = jnp.concatenate([jnp.zeros((n_in,), F32), b_gate, jnp.zeros((n_pad,), F32)])
    cos, sin = _rope_tables(s, dqk)
    w_out_b = w_out.astype(BF16)

    def project(x, mod):
        z = norm_proj(x, g, mod, w, bias=bias)
        gates_t = jnp.swapaxes(z[:, :, n_in:n_in + ngate], 1, 2)
        return z, gates_t

    def scan(z, gt, state, reverse, rotary):
        sx = z.shape[1]
        return mlstm_scan(z, gt, cos[:sx], sin[:sx], state, reverse=reverse, rotary=rotary,
                          heads=heads, dqk=dqk, dv=dv, chunk=min(ML_CHUNK, sx))

    def readout(hf, hb, z, x, mod):
        tm = _ptile(x.shape[1])
        o_blk = (2 * nqk + nv) // nv
        specs = [_row_spec(tm, nv), _row_spec(tm, nv), _row_spec(tm, nv, o_blk),
                 pl.BlockSpec((1, nv), lambda bi, i: (0, 0))]
        pro = functools.partial(_mlstm_readout_prologue, heads=heads, dv=dv)
        return resid_proj(pro, [hf, hb, z, head_g.reshape(1, nv)], specs, w_out_b, x, mod, gate_row=2, tm=tm)

    zero = (jnp.zeros((b, heads, dqk, dv), F32), jnp.zeros((b, heads, 1, dqk), F32),
            jnp.zeros((b, heads, 1, LANE), F32))
    zc, gtc = project(xc, mod_c)
    zl, gtl = project(xl, mod_l)
    hc_f, st_f = scan(zc, gtc, zero, False, False)
    hc_b, st_b = scan(zc, gtc, zero, True, False)
    hl_f, _ = scan(zl, gtl, st_f, False, True)
    hl_b, _ = scan(zl, gtl, st_b, True, True)
    yl = readout(hl_f, hl_b, zl, xl, mod_l)
    yc = readout(hc_f, hc_b, zc, xc, mod_c) if need_ctx else None
    return yl, yc


def na_mixer(xl, xc, g, mod_l, mod_c, w_qkv, q_g, k_g, rpb, w_o, need_ctx):
    d = xl.shape[2]
    wb, wob = w_qkv.astype(BF16), w_o.astype(BF16)
    zl = norm_proj(xl, g, mod_l, wb)
    zc = norm_proj(xc, g, mod_c, wb)
    ol = na_attention(zl, zc, q_g, k_g, rpb, heads=NA_HEADS)

    def out(o, x, mod):
        tm = _ptile(x.shape[1])
        return resid_proj(_cast_prologue, [o], [_row_spec(tm, d)], wob, x, mod, gate_row=2, tm=tm)

    yl = out(ol, xl, mod_l)
    yc = out(ctx_attention(zc, q_g, k_g, heads=NA_HEADS), xc, mod_c) if need_ctx else None
    return yl, yc


def conformer_mixer(x, g, mod, w_pw1, dw, dw_b, ln_g, ln_b, w_pw2):
    b, s, d = x.shape
    width = dw.shape[0]
    halo = 16
    assert width // 2 <= halo
    tm = _ptile(s)
    a = norm_proj(x, g, mod, w_pw1.astype(BF16), glu=True)
    vec = lambda: pl.BlockSpec((1, d), lambda bi, i: (0, 0))
    specs = _halo_specs(tm, halo, s, d) + [pl.BlockSpec((width, d), lambda bi, i: (0, 0)),
                                           vec(), vec(), vec()]
    pro = functools.partial(_conformer_prologue, tm=tm, width=width, halo=halo)
    return resid_proj(pro, [a, a, a, dw, dw_b.reshape(1, d), ln_g.reshape(1, d), ln_b.reshape(1, d)],
                      specs, w_pw2.astype(BF16), x, mod, gate_row=2, tm=tm,
                      scratch=[pltpu.VMEM((tm + 2 * halo, d), F32), pltpu.VMEM((tm, d), F32),
                               pltpu.VMEM((4, _conv_rows(tm, width)[1], LANE), F32)])


def rglru_mixer(xl, xc, g, mod_l, mod_c, w_in, conv_w, conv_b, w_gate, b_gate, lam, w_out, need_ctx):
    b, s, d = xl.shape
    r = w_out.shape[0]
    wb, wob, wgb = w_in.astype(BF16), w_out.astype(BF16), w_gate.astype(BF16)
    zl = norm_proj(xl, g, mod_l, wb)
    zc = norm_proj(xc, g, mod_c, wb)

    def scan(z, h0, reverse):
        k = 2 if reverse else 0
        return lru_scan(z, h0, conv_w, conv_b, wgb[k], wgb[k + 1], b_gate[k], b_gate[k + 1],
                        lam[1 if reverse else 0], reverse=reverse, tm=min(LRU_TILE, z.shape[1]))

    h0 = jnp.zeros((b, r), F32)
    hc_f = scan(zc, h0, False)
    hc_b = scan(zc, h0, True)
    hl_f = scan(zl, hc_f[:, -1], False)
    hl_b = scan(zl, hc_b[:, 0], True)

    def out(hf, hb, z, x, mod):
        tm = _ptile(x.shape[1])
        specs = [_row_spec(tm, r), _row_spec(tm, r), _row_spec(tm, r, 0)]
        return resid_proj(_lru_out_prologue, [hf, hb, z], specs, wob, x, mod, gate_row=2, tm=tm)

    yl = out(hl_f, hl_b, zl, xl, mod_l)
    yc = out(hc_f, hc_b, zc, xc, mod_c) if need_ctx else None
    return yl, yc


def kernel(x, c, ctx, c_ctx, norm_mix, norm_ffn, ada_w, ada_b, ml_w_in, ml_w_gate, ml_b_gate, ml_head_g, ml_w_out, na_w_qkv, na_q_g, na_k_g, na_rpb, na_w_o, cv_w_pw1, cv_dw, cv_dw_b, cv_ln_g, cv_ln_b, cv_w_pw2, lr_w_in, lr_conv, lr_conv_b, lr_w_gate, lr_b_gate, lr_lambda, lr_w_out, ffn_w_gu, ffn_conv, ffn_w_down):
    b, s, d = x.shape
    depth = ada_w.shape[0]
    n_mixers = 4
    cc_rows = -(-(b + 1) // SUBLANE) * SUBLANE
    cc = jnp.concatenate([c, c_ctx[None, :], jnp.zeros((cc_rows - b - 1, d), F32)], axis=0)
    mods = ada_modulation(cc, ada_w, ada_b).reshape(depth, cc_rows, 6, d)
    mods = jnp.pad(mods, ((0, 0), (0, 0), (0, MOD_ROWS - 6), (0, 0)))

    xl, xc = x, ctx
    for i in range(depth):
        kind, j = i % n_mixers, i // n_mixers
        need_ctx = i < depth - 1
        mod_l, mod_c = mods[i, :b], mods[i, b:b + 1]
        g = norm_mix[i]
        if kind == 0:
            xl, xc_new = mlstm_mixer(xl, xc, g, mod_l, mod_c, ml_w_in[j], ml_w_gate[j], ml_b_gate[j],
                                     ml_head_g[j], ml_w_out[j], need_ctx)
        elif kind == 1:
            xl, xc_new = na_mixer(xl, xc, g, mod_l, mod_c, na_w_qkv[j], na_q_g[j], na_k_g[j],
                                  na_rpb[j], na_w_o[j], need_ctx)
        elif kind == 2:
            cv = (cv_w_pw1[j], cv_dw[j], cv_dw_b[j], cv_ln_g[j], cv_ln_b[j], cv_w_pw2[j])
            xl, xc_new = conformer_mixer(xl, g, mod_l, *cv), None
            if need_ctx:
                xc_new = conformer_mixer(xc, g, mod_c, *cv)
        else:
            xl, xc_new = rglru_mixer(xl, xc, g, mod_l, mod_c, lr_w_in[j], lr_conv[j], lr_conv_b[j],
                                     lr_w_gate[j], lr_b_gate[j], lr_lambda[j], lr_w_out[j], need_ctx)
        wgu, wdn = ffn_w_gu[i].astype(BF16), ffn_w_down[i].astype(BF16)
        fc = 512 if wdn.shape[0] % 512 == 0 else LANE
        xl = conv_ffn(xl, norm_ffn[i], mod_l, wgu, ffn_conv[i], wdn, tm=ROW_TILE, fc=fc)
        if need_ctx:
            xc = conv_ffn(xc_new, norm_ffn[i], mod_c, wgu, ffn_conv[i], wdn, tm=ROW_TILE, fc=fc)
    return xl
```

```python
import functools

import numpy as np
import jax
import jax.numpy as jnp
from jax import lax
from jax.experimental import pallas as pl
from jax.experimental.pallas import tpu as pltpu

F32 = jnp.float32
BF16 = jnp.bfloat16

EPS = 1e-6
ROPE_THETA = 10000.0
GRID_W = 64
ML_HEADS = 8
ML_CHUNK = 256
ML_BATCH_PER_STEP = 2
NA_HEADS = 16
WIN_ROWS = 8
WIN_COLS = 16
NA_QROWS = 4
NA_HEADS_PER_STEP = 2
NA_QBLOCKS_PER_STEP = 8
LRU_BLOCKS = 8
LRU_C = 8.0
NEG = -1e30

V7X_VMEM_LIMIT = 56 * 1024 * 1024
LANE = 128
SUBLANE = 8
MOD_ROWS = 8
ROW_TILE = 512
PROJ_TILE = 256
PROJ_SUB = 128
MXU_WIDTH = 256
LRU_TILE = 256


def _cp(*sem):
    return pltpu.CompilerParams(dimension_semantics=sem, vmem_limit_bytes=V7X_VMEM_LIMIT)


def _dot(a, b):
    return jnp.dot(a, b, preferred_element_type=F32)


def _dot_nt(a, b):
    return lax.dot_general(a, b, (((1,), (1,)), ((), ())), preferred_element_type=F32)


def _sigmoid(x):
    return 1.0 / (1.0 + jnp.exp(-x))


def _log_sigmoid(x):
    return jnp.minimum(x, 0.0) - jnp.log(1.0 + jnp.exp(-jnp.abs(x)))


def _softplus(x):
    return jnp.maximum(x, 0.0) + jnp.log(1.0 + jnp.exp(-jnp.abs(x)))


def _rms(x, g):
    return x * lax.rsqrt(jnp.mean(x * x, axis=-1, keepdims=True) + EPS) * g


def _norm_mod(x, g, shift, scale):
    return _rms(x, g) * (1.0 + scale) + shift


def _pick(tile, n):
    assert n % tile == 0, (tile, n)
    return tile


def _ada_kernel(c_ref, w_ref, b_ref, o_ref):
    c = c_ref[...]
    s = (c * _sigmoid(c)).astype(BF16)
    o_ref[0] = _dot(s, w_ref[0].astype(BF16)) + b_ref[0]


def ada_modulation(cc, ada_w, ada_b):
    depth, d, n = ada_w.shape
    rows = cc.shape[0]
    tn = _pick(1024 if n % 1024 == 0 else n, n)
    return pl.pallas_call(
        _ada_kernel,
        grid=(depth, n // tn),
        in_specs=[pl.BlockSpec((rows, d), lambda l, j: (0, 0)),
                  pl.BlockSpec((1, d, tn), lambda l, j: (l, 0, j)),
                  pl.BlockSpec((1, 1, tn), lambda l, j: (l, 0, j))],
        out_specs=pl.BlockSpec((1, rows, tn), lambda l, j: (l, 0, j)),
        out_shape=jax.ShapeDtypeStruct((depth, rows, n), F32),
        compiler_params=_cp("parallel", "parallel"),
        name="ada_modulation",
    )(cc, ada_w, ada_b.reshape(depth, 1, n))


def _col_chunk(n):
    for c in (1536, 1280, 1024, 768, 512, 256, 128):
        if n % c == 0:
            return c
    return n


def _resident(shape):
    return pl.BlockSpec(shape, lambda *_: (0,) * len(shape), pipeline_mode=pl.Buffered(1))


def _proj_kernel(x_ref, g_ref, mod_ref, w_ref, *rest, glu, bias, sub, side):
    b_ref = rest[0] if bias else None
    o_ref = rest[1 if bias else 0]
    side_ref = rest[-1] if side is not None else None
    tm, n = o_ref.shape[1], o_ref.shape[2]
    nchunk = _col_chunk(n)
    shift = mod_ref[0, 0:1, :]
    gain = g_ref[...] * (1.0 + mod_ref[0, 1:2, :])
    for r0 in range(0, tm, sub):
        x = x_ref[0, r0:r0 + sub, :]
        h = (x * lax.rsqrt(jnp.mean(x * x, axis=-1, keepdims=True) + EPS) * gain + shift).astype(BF16)
        for c0 in range(0, n, nchunk):
            y = _dot(h, w_ref[:, c0:c0 + nchunk])
            if glu:
                y = y * _sigmoid(_dot(h, w_ref[:, n + c0:n + c0 + nchunk]))
            if bias:
                y = y + b_ref[:, c0:c0 + nchunk]
            o_ref[0, r0:r0 + sub, c0:c0 + nchunk] = y
            if side is not None and c0 <= side < c0 + nchunk:
                side_ref[0, r0:r0 + sub, :] = y[:, side - c0:side - c0 + LANE]


def norm_proj(x, g, mod, w, *, bias=None, glu=False, side=None):
    b, s, d = x.shape
    n = w.shape[1] // 2 if glu else w.shape[1]
    tm = min(PROJ_TILE, s)
    sub = min(PROJ_SUB, tm)
    assert s % tm == 0 and tm % sub == 0
    per_batch = mod.shape[0] != 1
    in_specs = [pl.BlockSpec((1, tm, d), lambda bi, i: (bi, i, 0)),
                pl.BlockSpec((1, d), lambda bi, i: (0, 0)),
                pl.BlockSpec((1, MOD_ROWS, d), lambda bi, i: (bi if per_batch else 0, 0, 0)),
                _resident(w.shape)]
    args = [x, g.reshape(1, d), mod, w]
    if bias is not None:
        in_specs.append(pl.BlockSpec((1, n), lambda bi, i: (0, 0)))
        args.append(bias.reshape(1, n))
    out_specs = [pl.BlockSpec((1, tm, n), lambda bi, i: (bi, i, 0))]
    out_shape = [jax.ShapeDtypeStruct((b, s, n), F32)]
    if side is not None:
        assert side % LANE == 0 and side // _col_chunk(n) == (side + LANE - 1) // _col_chunk(n)
        out_specs.append(pl.BlockSpec((1, tm, LANE), lambda bi, i: (bi, i, 0)))
        out_shape.append(jax.ShapeDtypeStruct((b, s, LANE), F32))
    outs = pl.pallas_call(
        functools.partial(_proj_kernel, glu=glu, bias=bias is not None, sub=sub, side=side),
        grid=(b, s // tm),
        in_specs=in_specs,
        out_specs=out_specs,
        out_shape=out_shape,
        compiler_params=_cp("parallel", "parallel"),
        name="norm_proj",
    )(*args)
    return outs[0] if side is None else outs


def _resid_kernel(*refs, prologue, n_in, gate_row, sub):
    in_refs = refs[:n_in]
    w_ref, x_ref, mod_ref, o_ref, *scr = refs[n_in:]
    pos = (pl.program_id(1), pl.num_programs(1))
    tm, n = x_ref.shape[1], x_ref.shape[2]
    nchunk = _col_chunk(n)
    gate = mod_ref[0, gate_row:gate_row + 1, :]
    for r0 in range(0, tm, sub):
        h = prologue(pos, r0, sub, *in_refs, *scr)
        for c0 in range(0, n, nchunk):
            y = _dot(h, w_ref[:, c0:c0 + nchunk])
            o_ref[0, r0:r0 + sub, c0:c0 + nchunk] = (x_ref[0, r0:r0 + sub, c0:c0 + nchunk]
                                                    + gate[:, c0:c0 + nchunk] * y)


def resid_proj(prologue, ins, in_specs, w, x, mod, *, gate_row, tm, scratch=()):
    b, s, n = x.shape
    sub = min(PROJ_SUB, tm)
    assert s % tm == 0 and tm % sub == 0
    per_batch = mod.shape[0] != 1
    specs = list(in_specs) + [
        _resident(w.shape),
        pl.BlockSpec((1, tm, n), lambda bi, i: (bi, i, 0)),
        pl.BlockSpec((1, MOD_ROWS, n), lambda bi, i: (bi if per_batch else 0, 0, 0))]
    return pl.pallas_call(
        functools.partial(_resid_kernel, prologue=prologue, n_in=len(ins), gate_row=gate_row, sub=sub),
        grid=(b, s // tm),
        in_specs=specs,
        out_specs=pl.BlockSpec((1, tm, n), lambda bi, i: (bi, i, 0)),
        out_shape=jax.ShapeDtypeStruct((b, s, n), F32),
        scratch_shapes=list(scratch),
        compiler_params=_cp("parallel", "parallel"),
        name="resid_proj",
    )(*ins, w, x, mod)


def _ffn_kernel(x_ref, xp_ref, xn_ref, g_ref, mod_ref, wg_ref, wu_ref, cw_ref, wd_ref, o_ref,
                h_scr, g_scr, acc_scr, *, tm, seq):
    i, j = pl.program_id(1), pl.program_id(2)
    nt, nf = pl.num_programs(1), pl.num_programs(2)
    hb = SUBLANE

    @pl.when(j == 0)
    def _():
        g, shift, scale = g_ref[...], mod_ref[0, 3:4, :], mod_ref[0, 4:5, :]
        h_scr[0:tm] = _norm_mod(x_ref[0], g, shift, scale).astype(BF16)
        halo = jnp.concatenate([xp_ref[0], xn_ref[0]], axis=0)
        h_scr[tm:] = _norm_mod(halo, g, shift, scale).astype(BF16)
        acc_scr[...] = jnp.zeros_like(acc_scr)

    ga = _dot(h_scr[...], wg_ref[...])
    uu = _dot(h_scr[0:tm], wu_ref[...])
    gg = ga[0:tm]
    g_scr[0:hb] = jnp.where(i > 0, ga[tm:tm + hb], 0.0)
    g_scr[hb:hb + tm] = gg
    g_scr[hb + tm:] = jnp.where(i < nt - 1, ga[tm + hb:], 0.0)
    cw = cw_ref[...]
    g_prev, g_next = g_scr[pl.ds(hb - 1, tm), :], g_scr[pl.ds(hb + 1, tm), :]
    if seq < tm:
        pos = lax.broadcasted_iota(jnp.int32, (tm, 1), 0) % seq
        g_prev = jnp.where(pos > 0, g_prev, 0.0)
        g_next = jnp.where(pos < seq - 1, g_next, 0.0)
    gc = cw[0:1] * g_prev + cw[1:2] * gg + cw[2:3] * g_next
    act = (gc * _sigmoid(gc) * uu).astype(BF16)
    acc_scr[...] += _dot(act, wd_ref[...])

    @pl.when(j == nf - 1)
    def _():
        o_ref[0] = x_ref[0] + mod_ref[0, 5:6, :] * acc_scr[...]


def conv_ffn(x, g, mod, w_gu, conv_w, w_down, *, tm, fc):
    seq = x.shape[1]
    per_batch = mod.shape[0] != 1
    if not per_batch and tm % seq == 0:
        out = _conv_ffn(x.reshape(1, -1, x.shape[2]), g, mod, w_gu, conv_w, w_down, tm=tm, fc=fc, seq=seq)
        return out.reshape(x.shape)
    return _conv_ffn(x, g, mod, w_gu, conv_w, w_down, tm=tm, fc=fc, seq=seq)


def _conv_ffn(x, g, mod, w_gu, conv_w, w_down, *, tm, fc, seq):
    b, s, d = x.shape
    f = w_down.shape[0]
    tm, fc = _pick(min(tm, s), s), _pick(fc, f)
    assert seq == s or tm % seq == 0
    per_batch = mod.shape[0] != 1
    nh = tm // SUBLANE
    last_h = s // SUBLANE - 1
    return pl.pallas_call(
        functools.partial(_ffn_kernel, tm=tm, seq=seq),
        grid=(b, s // tm, f // fc),
        in_specs=[
            pl.BlockSpec((1, tm, d), lambda bi, i, j: (bi, i, 0)),
            pl.BlockSpec((1, SUBLANE, d), lambda bi, i, j: (bi, jnp.maximum(i * nh - 1, 0), 0)),
            pl.BlockSpec((1, SUBLANE, d), lambda bi, i, j: (bi, jnp.minimum((i + 1) * nh, last_h), 0)),
            pl.BlockSpec((1, d), lambda bi, i, j: (0, 0)),
            pl.BlockSpec((1, MOD_ROWS, d), lambda bi, i, j: (bi if per_batch else 0, 0, 0)),
            pl.BlockSpec((d, fc), lambda bi, i, j: (0, j)),
            pl.BlockSpec((d, fc), lambda bi, i, j: (0, j + f // fc)),
            pl.BlockSpec((conv_w.shape[0], fc), lambda bi, i, j: (0, j)),
            pl.BlockSpec((fc, d), lambda bi, i, j: (j, 0)),
        ],
        out_specs=pl.BlockSpec((1, tm, d), lambda bi, i, j: (bi, i, 0)),
        out_shape=jax.ShapeDtypeStruct((b, s, d), F32),
        scratch_shapes=[pltpu.VMEM((tm + 2 * SUBLANE, d), BF16),
                        pltpu.VMEM((tm + 2 * SUBLANE, fc), F32),
                        pltpu.VMEM((tm, d), F32)],
        compiler_params=_cp("parallel", "parallel", "arbitrary"),
        name="conv_ffn",
    )(x, x, x, g.reshape(1, d), mod, w_gu, w_gu, conv_w, w_down)


def _split3(x):
    hi = x.astype(BF16)
    r = x - hi.astype(F32)
    mid = r.astype(BF16)
    lo = (r - mid.astype(F32)).astype(BF16)
    return hi, mid, lo


def _rope_perm(d):
    half = d // 4
    src = lax.broadcasted_iota(jnp.int32, (d, d), 0)
    dst = lax.broadcasted_iota(jnp.int32, (d, d), 1)
    partner = jnp.where((dst % (2 * half)) < half, dst + half, dst - half)
    return jnp.where(src == partner, 1.0, 0.0).astype(BF16)


def _rope(x, cos, sin_signed, perm):
    return x * cos + _dot(x.astype(BF16), perm) * sin_signed


def _mlstm_kernel(q_ref, k_ref, v_ref, gc_ref, gr_ref, cos_ref, sin_ref, c0_ref, n0_ref, m0_ref,
                  h_ref, c_out, n_out, m_out, c_scr, n_scr, m_scr, *, reverse, rotary, heads, dqk, dv):
    c = pl.program_id(1)
    nc = pl.num_programs(1)
    nb, L = q_ref.shape[0], q_ref.shape[1]

    @pl.when(c == 0)
    def _():
        c_scr[...] = c0_ref[...]
        n_scr[...] = n0_ref[...]
        m_scr[...] = m0_ref[...]

    off_i = 2 * heads if reverse else 0
    off_f = off_i + heads
    lane = lax.broadcasted_iota(jnp.int32, (L, gc_ref.shape[2]), 1)

    row = lax.broadcasted_iota(jnp.int32, (L, L), 0)
    col = lax.broadcasted_iota(jnp.int32, (L, L), 1)
    mask = (col >= row) if reverse else (col <= row)
    mask_t = (row >= col) if reverse else (row <= col)
    tri = jnp.where(mask, 1.0, 0.0).astype(BF16)
    tri_t = jnp.where(mask_t, 1.0, 0.0).astype(BF16)
    perm = _rope_perm(dqk) if rotary else None

    for bb in range(nb):
        gcol = gc_ref[bb]
        grow = gr_ref[bb]
        lf_rows = _log_sigmoid(grow)
        b_all = sum(_dot(tri, p) for p in _split3(_log_sigmoid(gcol)))
        b_rows = sum(_dot(p, tri_t) for p in _split3(lf_rows))
        for hd in range(heads):
            qs, vs = slice(hd * dqk, (hd + 1) * dqk), slice(hd * dv, (hd + 1) * dv)
            q = q_ref[bb, :, qs] * (dqk ** -0.5)
            k = k_ref[bb, :, qs]
            if rotary:
                q = _rope(q, cos_ref[...], sin_ref[...], perm)
                k = _rope(k, cos_ref[...], sin_ref[...], perm)
            qb, kb, vb = q.astype(BF16), k.astype(BF16), v_ref[bb, :, vs].astype(BF16)

            li_col = jnp.sum(jnp.where(lane == off_i + hd, gcol, 0.0), axis=-1, keepdims=True)
            b_col = jnp.sum(jnp.where(lane == off_f + hd, b_all, 0.0), axis=-1, keepdims=True)
            li_row = grow[off_i + hd:off_i + hd + 1]
            b_row = b_rows[off_f + hd:off_f + hd + 1]
            total = jnp.sum(lf_rows[off_f + hd:off_f + hd + 1], axis=-1, keepdims=True)

            m_prev = m_scr[bb, hd, 0:1, 0:1]
            rmat = jnp.where(mask, li_row - b_row, NEG)
            g_col = jnp.maximum(m_prev, jnp.max(rmat, axis=-1, keepdims=True))
            m_t = b_col + g_col
            w_inter = jnp.exp(m_prev - g_col)
            s_qk = _dot_nt(qb, kb) * jnp.exp(rmat - g_col)
            num = w_inter * _dot(qb, c_scr[bb, hd].astype(BF16)) + _dot(s_qk.astype(BF16), vb)
            den = (w_inter * jnp.sum(q * n_scr[bb, hd], axis=-1, keepdims=True)
                   + jnp.sum(s_qk, axis=-1, keepdims=True))
            h_ref[bb, :, vs] = num / jnp.maximum(jnp.abs(den), jnp.exp(-m_t))

            w_col = total - b_col + li_col
            m_new = jnp.maximum(total + m_prev, jnp.max(w_col, axis=0, keepdims=True))
            decay = jnp.exp(total + m_prev - m_new)
            ek = k * jnp.exp(w_col - m_new)
            c_scr[bb, hd] = decay * c_scr[bb, hd] + _dot(ek.T.astype(BF16), vb)
            n_scr[bb, hd] = decay * n_scr[bb, hd] + jnp.sum(ek, axis=0, keepdims=True)
            m_scr[bb, hd] = jnp.broadcast_to(m_new, (1, m_scr.shape[3]))

    @pl.when(c == nc - 1)
    def _():
        c_out[...] = c_scr[...]
        n_out[...] = n_scr[...]
        m_out[...] = m_scr[...]


def mlstm_scan(z, gates, cos, sin, state, *, reverse, rotary, heads, dqk, dv, chunk):
    b, s, _ = z.shape
    L = _pick(chunk, s)
    nc = s // L
    nqk, nv = heads * dqk, heads * dv
    assert (2 * nqk) % nv == 0
    gates_c, gates_t = gates

    def ci(c):
        return nc - 1 - c if reverse else c

    c0, n0, m0 = state
    nb = ML_BATCH_PER_STEP if b % ML_BATCH_PER_STEP == 0 else 1
    st_specs = [pl.BlockSpec((nb, heads, dqk, dv), lambda bi, c: (bi, 0, 0, 0)),
                pl.BlockSpec((nb, heads, 1, dqk), lambda bi, c: (bi, 0, 0, 0)),
                pl.BlockSpec((nb, heads, 1, LANE), lambda bi, c: (bi, 0, 0, 0))]
    h, c1, n1, m1 = pl.pallas_call(
        functools.partial(_mlstm_kernel, reverse=reverse, rotary=rotary, heads=heads, dqk=dqk, dv=dv),
        grid=(b // nb, nc),
        in_specs=[
            pl.BlockSpec((nb, L, nqk), lambda bi, c: (bi, ci(c), 0)),
            pl.BlockSpec((nb, L, nqk), lambda bi, c: (bi, ci(c), 1)),
            pl.BlockSpec((nb, L, nv), lambda bi, c: (bi, ci(c), 2 * nqk // nv)),
            pl.BlockSpec((nb, L, LANE), lambda bi, c: (bi, ci(c), 0)),
            pl.BlockSpec((nb, 4 * heads, L), lambda bi, c: (bi, 0, ci(c))),
            pl.BlockSpec((L, dqk), lambda bi, c: (ci(c), 0)),
            pl.BlockSpec((L, dqk), lambda bi, c: (ci(c), 0)),
        ] + st_specs,
        out_specs=[pl.BlockSpec((nb, L, nv), lambda bi, c: (bi, ci(c), 0))] + st_specs,
        out_shape=[jax.ShapeDtypeStruct((b, s, nv), F32),
                   jax.ShapeDtypeStruct(c0.shape, F32),
                   jax.ShapeDtypeStruct(n0.shape, F32),
                   jax.ShapeDtypeStruct(m0.shape, F32)],
        scratch_shapes=[pltpu.VMEM((nb, heads, dqk, dv), F32), pltpu.VMEM((nb, heads, 1, dqk), F32),
                        pltpu.VMEM((nb, heads, 1, LANE), F32)],
        compiler_params=_cp("parallel", "arbitrary"),
        name="mlstm_scan",
    )(z, z, z, gates_c, gates_t, cos, sin, c0, n0, m0)
    return h, (c1, n1, m1)


def _rope_tables(s, d):
    half, quarter = d // 2, d // 4
    freqs = ROPE_THETA ** (-jnp.arange(0, half, 2, dtype=F32) / half)
    t = jnp.arange(s)
    ang_r = (t // GRID_W).astype(F32)[:, None] * freqs[None, :]
    ang_c = (t % GRID_W).astype(F32)[:, None] * freqs[None, :]
    cos = jnp.concatenate([jnp.cos(ang_r)] * 2 + [jnp.cos(ang_c)] * 2, axis=-1)
    sin = jnp.concatenate([-jnp.sin(ang_r), jnp.sin(ang_r), -jnp.sin(ang_c), jnp.sin(ang_c)], axis=-1)
    del quarter
    return cos, sin


def _mlstm_readout_prologue(pos, r0, rows, hf_ref, hb_ref, o_ref, g_ref, *, heads, dv):
    rs = slice(r0, r0 + rows)
    parts = []
    for hd in range(heads):
        sl = slice(hd * dv, (hd + 1) * dv)
        hs = hf_ref[0, rs, sl] + hb_ref[0, rs, sl]
        parts.append((_rms(hs, g_ref[:, sl]) * _sigmoid(o_ref[0, rs, sl])).astype(BF16))
    return jnp.concatenate(parts, axis=-1)


def _na_kernel(q_ref, k_ref, v_ref, kc_ref, vc_ref, qg_ref, kg_ref, bias_ref, o_ref,
               kn_scr, vn_scr, kcn_scr, vcn_scr, *, nkeys, kstep, half, kr_max, hps, dh, rps):
    nb = pl.num_programs(2) * rps
    tq = q_ref.shape[1] // rps

    @pl.when(pl.program_id(2) == 0)
    def _():
        for hh in range(hps):
            sl = slice(hh * dh, (hh + 1) * dh)
            kn_scr[hh] = _rms(k_ref[0, :, sl], kg_ref[...]).astype(BF16)
            kcn_scr[hh] = _rms(kc_ref[0, :, sl], kg_ref[...]).astype(BF16)
            vn_scr[hh, :, 0:dh] = v_ref[0, :, sl].astype(BF16)
            vn_scr[hh, :, dh:] = jnp.ones((vn_scr.shape[1], dh), BF16)
            vcn_scr[hh, :, 0:dh] = vc_ref[0, :, sl].astype(BF16)
            vcn_scr[hh, :, dh:] = jnp.ones((vcn_scr.shape[1], dh), BF16)

    for rr in range(rps):
        rb = pl.program_id(2) * rps + rr
        rows = slice(rr * tq, (rr + 1) * tq)
        kr0 = jnp.clip(rb * (tq // kstep) - half, 0, kr_max)
        start = pl.multiple_of(kr0 * kstep, kstep)
        cls = jnp.where(rb == 0, 0, jnp.where(rb == nb - 1, 2, 1))
        for hh in range(hps):
            sl = slice(hh * dh, (hh + 1) * dh)
            qn = (_rms(q_ref[0, rows, sl], qg_ref[...]) * (dh ** -0.5)).astype(BF16)
            s_loc = _dot_nt(qn, kn_scr[hh, pl.ds(start, nkeys), :]) + bias_ref[hh, cls]
            s_ctx = _dot_nt(qn, kcn_scr[hh])
            m = jnp.maximum(jnp.max(s_loc, axis=-1, keepdims=True), jnp.max(s_ctx, axis=-1, keepdims=True))
            p_loc = jnp.exp(s_loc - m).astype(BF16)
            p_ctx = jnp.exp(s_ctx - m).astype(BF16)
            ol = _dot(p_loc, vn_scr[hh, pl.ds(start, nkeys), :]) + _dot(p_ctx, vcn_scr[hh])
            o_ref[0, rows, sl] = ol[:, 0:dh] / ol[:, dh:dh + 1]


def _na_bias_tables(rows):
    W, R, half = GRID_W, NA_QROWS, WIN_ROWS // 2
    win_r = min(WIN_ROWS, rows)
    U = R + win_r - 1
    nb = rows // R
    kr_max = rows - U
    assert nb >= 3 and R >= half and kr_max >= 0
    ri, kj = np.arange(R)[:, None], np.arange(U)[None, :]
    per_block = []
    for rb in range(nb):
        r = rb * R + ri
        kr = np.clip(rb * R - half, 0, kr_max) + kj
        rs = np.clip(r - half, 0, rows - win_r)
        ok = (kr >= rs) & (kr < rs + win_r)
        dr = np.clip(kr - r + WIN_ROWS - 1, 0, 2 * WIN_ROWS - 2)
        per_block.append((ok, dr))
    for rb in range(2, nb - 1):
        assert all(np.array_equal(a, b_) for a, b_ in zip(per_block[1], per_block[rb]))
    return [per_block[0], per_block[1], per_block[nb - 1]], U, kr_max


def _na_bias_kernel(rpb_ref, o_ref, t_scr, *, classes):
    h = pl.program_id(0)
    W = t_scr.shape[1]
    ndr, ndc = 2 * WIN_ROWS - 1, 2 * WIN_COLS - 1
    c = lax.broadcasted_iota(jnp.int32, (W, W), 0)
    kc = lax.broadcasted_iota(jnp.int32, (W, W), 1)
    cstart = jnp.clip(c - WIN_COLS // 2, 0, W - WIN_COLS)
    dcidx = jnp.clip(kc - c + WIN_COLS - 1, 0, ndc - 1)
    for dr in range(ndr):
        t = jnp.full((W, W), NEG, F32)
        for dc in range(ndc):
            t = jnp.where(dcidx == dc, rpb_ref[h, dr * ndc + dc], t)
        t = jnp.where(kc >= cstart, t, NEG)
        t_scr[dr] = jnp.where(kc < cstart + WIN_COLS, t, NEG)
    o_ref[...] = jnp.full(o_ref.shape, NEG, F32)
    for cls, (ok, dr) in enumerate(classes):
        for ri in range(ok.shape[0]):
            for kj in range(ok.shape[1]):
                if ok[ri, kj]:
                    o_ref[0, cls, ri * W:(ri + 1) * W, kj * W:(kj + 1) * W] = t_scr[int(dr[ri, kj])]


def na_bias_table(rpb, classes, tq, nkeys):
    heads = rpb.shape[0]
    return pl.pallas_call(
        functools.partial(_na_bias_kernel, classes=classes),
        grid=(heads,),
        in_specs=[pl.BlockSpec(memory_space=pltpu.SMEM)],
        out_specs=pl.BlockSpec((1, len(classes), tq, nkeys), lambda h: (h, 0, 0, 0)),
        out_shape=jax.ShapeDtypeStruct((heads, len(classes), tq, nkeys), F32),
        scratch_shapes=[pltpu.VMEM((2 * WIN_ROWS - 1, GRID_W, GRID_W), F32)],
        compiler_params=_cp("parallel"),
        name="na_bias_table",
    )(rpb.reshape(heads, -1))


def na_attention(z, zc, q_g, k_g, rpb, *, heads):
    b, s, n3 = z.shape
    dh = n3 // (3 * heads)
    cl = zc.shape[1]
    rows = s // GRID_W
    classes, U, kr_max = _na_bias_tables(rows)
    tq, nkeys = NA_QROWS * GRID_W, U * GRID_W
    bias = na_bias_table(rpb, classes, tq, nkeys)
    hps, rps = NA_HEADS_PER_STEP, NA_QBLOCKS_PER_STEP
    hg, w = heads // hps, hps * dh
    assert (s // tq) % rps == 0
    return pl.pallas_call(
        functools.partial(_na_kernel, nkeys=nkeys, kstep=GRID_W, half=WIN_ROWS // 2, kr_max=kr_max,
                          hps=hps, dh=dh, rps=rps),
        grid=(b, hg, s // (tq * rps)),
        in_specs=[
            pl.BlockSpec((1, tq * rps, w), lambda bi, h, r: (bi, r, h)),
            pl.BlockSpec((1, s, w), lambda bi, h, r: (bi, 0, hg + h)),
            pl.BlockSpec((1, s, w), lambda bi, h, r: (bi, 0, 2 * hg + h)),
            pl.BlockSpec((1, cl, w), lambda bi, h, r: (bi, 0, hg + h)),
            pl.BlockSpec((1, cl, w), lambda bi, h, r: (bi, 0, 2 * hg + h)),
            pl.BlockSpec((1, dh), lambda bi, h, r: (0, 0)),
            pl.BlockSpec((1, dh), lambda bi, h, r: (0, 0)),
            pl.BlockSpec((hps, 3, tq, nkeys), lambda bi, h, r: (h, 0, 0, 0)),
        ],
        out_specs=pl.BlockSpec((1, tq * rps, w), lambda bi, h, r: (bi, r, h)),
        out_shape=jax.ShapeDtypeStruct((b, s, heads * dh), F32),
        scratch_shapes=[pltpu.VMEM((hps, s, dh), BF16), pltpu.VMEM((hps, s, 2 * dh), BF16),
                        pltpu.VMEM((hps, cl, dh), BF16), pltpu.VMEM((hps, cl, 2 * dh), BF16)],
        compiler_params=_cp("parallel", "parallel", "arbitrary"),
        name="na_attention",
    )(z, z, z, zc, zc, q_g.reshape(1, dh), k_g.reshape(1, dh), bias)


def _ctx_attn_kernel(q_ref, k_ref, v_ref, qg_ref, kg_ref, o_ref):
    dh = q_ref.shape[2]
    qn = (_rms(q_ref[0], qg_ref[...]) * (dh ** -0.5)).astype(BF16)
    kn = _rms(k_ref[0], kg_ref[...]).astype(BF16)
    s = _dot_nt(qn, kn)
    p = jnp.exp(s - jnp.max(s, axis=-1, keepdims=True))
    l = jnp.sum(p, axis=-1, keepdims=True)
    o_ref[0] = _dot(p.astype(BF16), v_ref[0].astype(BF16)) / l


def ctx_attention(zc, q_g, k_g, *, heads):
    b, cl, n3 = zc.shape
    dh = n3 // (3 * heads)
    return pl.pallas_call(
        _ctx_attn_kernel,
        grid=(b, heads),
        in_specs=[pl.BlockSpec((1, cl, dh), lambda bi, h: (bi, 0, h)),
                  pl.BlockSpec((1, cl, dh), lambda bi, h: (bi, 0, heads + h)),
                  pl.BlockSpec((1, cl, dh), lambda bi, h: (bi, 0, 2 * heads + h)),
                  pl.BlockSpec((1, dh), lambda bi, h: (0, 0)),
                  pl.BlockSpec((1, dh), lambda bi, h: (0, 0))],
        out_specs=pl.BlockSpec((1, cl, dh), lambda bi, h: (bi, 0, h)),
        out_shape=jax.ShapeDtypeStruct((b, cl, heads * dh), F32),
        compiler_params=_cp("parallel", "parallel"),
        name="ctx_attention",
    )(zc, zc, zc, q_g.reshape(1, dh), k_g.reshape(1, dh))


def _cast_prologue(pos, r0, rows, a_ref):
    return a_ref[0, r0:r0 + rows, :].astype(BF16)


def _conv_rows(tm, width):
    rsub = min(tm, PROJ_SUB)
    return rsub, rsub + (-(-width // SUBLANE) - 1) * SUBLANE


def _conformer_prologue(pos, r0, rows, a_ref, ap_ref, an_ref, dw_ref, dwb_ref, lg_ref, lb_ref, ext_scr, y_scr,
                        sh_scr, *, tm, width, halo):
    i, nt = pos
    d = a_ref.shape[2]
    pad = width // 2
    if r0 == 0:
        ext_scr[0:halo] = jnp.where(i > 0, ap_ref[0], 0.0)
        ext_scr[halo:halo + tm] = a_ref[0]
        ext_scr[halo + tm:] = jnp.where(i < nt - 1, an_ref[0], 0.0)

    rsub, span = _conv_rows(tm, width)
    assert rows == rsub
    n_a = -(-width // SUBLANE)

    def col_block(cb, carry):
        cs = pl.ds(pl.multiple_of(cb * LANE, LANE), LANE)
        acc = jnp.broadcast_to(dwb_ref[:, cs], (rsub, LANE))
        for b in range(SUBLANE):
            slot = b % sh_scr.shape[0]
            sh_scr[slot] = ext_scr[pl.ds(halo - pad + r0 + b, span), cs]
            for a in range(n_a):
                k = a * SUBLANE + b
                if k < width:
                    acc = acc + dw_ref[k:k + 1, cs] * sh_scr[slot, a * SUBLANE:a * SUBLANE + rsub, :]
        y_scr[r0:r0 + rsub, cs] = acc
        return carry

    lax.fori_loop(0, d // LANE, col_block, 0)
    y = y_scr[r0:r0 + rsub, :]
    mu = jnp.mean(y, axis=-1, keepdims=True)
    var = jnp.mean(jnp.square(y - mu), axis=-1, keepdims=True)
    yn = (y - mu) * lax.rsqrt(var + EPS) * lg_ref[...] + lb_ref[...]
    return (yn * _sigmoid(yn)).astype(BF16)


def _halo_specs(tm, halo, s, width_cols, col_blk=0):
    nh = tm // halo
    last = s // halo - 1
    return [pl.BlockSpec((1, tm, width_cols), lambda bi, i: (bi, i, col_blk)),
            pl.BlockSpec((1, halo, width_cols), lambda bi, i: (bi, jnp.maximum(i * nh - 1, 0), col_blk)),
            pl.BlockSpec((1, halo, width_cols), lambda bi, i: (bi, jnp.minimum((i + 1) * nh, last), col_blk))]


def _lru_kernel(u_ref, up_ref, un_ref, cw_ref, cb_ref, wr_ref, wi_ref, br_ref, bi_ref, lam_ref, h0_ref,
                o_ref, ext_scr, a_scr, b_scr, carry_scr, *, tm, reverse, blocks, halo):
    i, nt = pl.program_id(1), pl.num_programs(1)
    d = u_ref.shape[2]
    bw = d // blocks
    first = (i == 0)
    ti = (nt - 1 - i) if reverse else i

    @pl.when(first)
    def _():
        carry_scr[...] = h0_ref[0]

    ext_scr[0:halo] = jnp.where(ti > 0, up_ref[0], 0.0)
    ext_scr[halo:halo + tm] = u_ref[0]
    ext_scr[halo + tm:] = jnp.where(ti < nt - 1, un_ref[0], 0.0)
    width = cw_ref.shape[0]
    pad_l = width // 2
    u = jnp.broadcast_to(cb_ref[...], (tm, d))
    ext = ext_scr[...]
    for k in range(width):
        shifted = ext if k == pad_l else pltpu.roll(ext, (pad_l - k) % ext.shape[0], 0)
        u = u + cw_ref[k:k + 1, :] * shifted[halo:halo + tm]

    ub = u.astype(BF16)
    neg_c_sp = -LRU_C * _softplus(-lam_ref[...])
    for blk in range(blocks):
        sl = slice(blk * bw, (blk + 1) * bw)
        r = _sigmoid(_dot(ub[:, sl], wr_ref[blk]) + br_ref[:, sl])
        g = _sigmoid(_dot(ub[:, sl], wi_ref[blk]) + bi_ref[:, sl])
        log_a = neg_c_sp[:, sl] * r
        a = jnp.exp(log_a)
        a_scr[:, sl] = a
        b_scr[:, sl] = jnp.sqrt(1.0 - a * a) * (g * u[:, sl])

    sub = lax.broadcasted_iota(jnp.int32, (SUBLANE, d), 0)
    ngroups = tm // SUBLANE

    def group(gi, carry):
        r0 = pl.multiple_of(((ngroups - 1 - gi) if reverse else gi) * SUBLANE, SUBLANE)
        a8 = a_scr[pl.ds(r0, SUBLANE), :]
        b8 = b_scr[pl.ds(r0, SUBLANE), :]
        for sft in (1, 2, 4):
            if reverse:
                valid = sub < SUBLANE - sft
                a_sh = pltpu.roll(a8, SUBLANE - sft, 0)
                b_sh = pltpu.roll(b8, SUBLANE - sft, 0)
            else:
                valid = sub >= sft
                a_sh = pltpu.roll(a8, sft, 0)
                b_sh = pltpu.roll(b8, sft, 0)
            b8 = jnp.where(valid, a8 * b_sh + b8, b8)
            a8 = jnp.where(valid, a8 * a_sh, a8)
        h8 = a8 * carry + b8
        o_ref[0, pl.ds(r0, SUBLANE), :] = h8
        return h8[0:1, :] if reverse else h8[SUBLANE - 1:SUBLANE, :]

    carry_scr[...] = lax.fori_loop(0, ngroups, group, carry_scr[...])


def lru_scan(z, h0, conv_w, conv_b, w_r, w_i, b_r, b_i, lam, *, reverse, tm):
    b, s, r2 = z.shape
    r = r2 // 2
    tm = _pick(tm, s)
    nt = s // tm
    halo = SUBLANE
    nh = tm // halo
    last = s // halo - 1

    def ti(i):
        return nt - 1 - i if reverse else i

    vec = lambda: pl.BlockSpec((1, r), lambda bi, i: (0, 0))
    return pl.pallas_call(
        functools.partial(_lru_kernel, tm=tm, reverse=reverse, blocks=w_r.shape[0], halo=halo),
        grid=(b, nt),
        in_specs=[
            pl.BlockSpec((1, tm, r), lambda bi, i: (bi, ti(i), 1)),
            pl.BlockSpec((1, halo, r), lambda bi, i: (bi, jnp.maximum(ti(i) * nh - 1, 0), 1)),
            pl.BlockSpec((1, halo, r), lambda bi, i: (bi, jnp.minimum((ti(i) + 1) * nh, last), 1)),
            pl.BlockSpec(conv_w.shape, lambda bi, i: (0, 0)),
            vec(),
            pl.BlockSpec(w_r.shape, lambda bi, i: (0, 0, 0)),
            pl.BlockSpec(w_i.shape, lambda bi, i: (0, 0, 0)),
            vec(), vec(), vec(),
            pl.BlockSpec((1, 1, r), lambda bi, i: (bi, 0, 0)),
        ],
        out_specs=pl.BlockSpec((1, tm, r), lambda bi, i: (bi, ti(i), 0)),
        out_shape=jax.ShapeDtypeStruct((b, s, r), F32),
        scratch_shapes=[pltpu.VMEM((tm + 2 * halo, r), F32), pltpu.VMEM((tm, r), F32),
                        pltpu.VMEM((tm, r), F32), pltpu.VMEM((1, r), F32)],
        compiler_params=_cp("parallel", "arbitrary"),
        name="lru_scan",
    )(z, z, z, conv_w, conv_b.reshape(1, r), w_r, w_i, b_r.reshape(1, r), b_i.reshape(1, r),
      lam.reshape(1, r), h0.reshape(b, 1, r))


def _lru_out_prologue(pos, r0, rows, hf_ref, hb_ref, gate_ref):
    rs = slice(r0, r0 + rows)
    g = gate_ref[0, rs, :]
    gelu = 0.5 * g * (1.0 + jnp.tanh(0.7978845608028654 * (g + 0.044715 * (g * g * g))))
    return ((hf_ref[0, rs, :] + hb_ref[0, rs, :]) * gelu).astype(BF16)


def _row_spec(tm, cols, col_blk=0):
    return pl.BlockSpec((1, tm, cols), lambda bi, i: (bi, i, col_blk))


def _tiles(s):
    return min(ROW_TILE, s)


def _ptile(s):
    return min(PROJ_TILE, s)


def mlstm_mixer(xl, xc, g, mod_l, mod_c, w_in, w_gate, b_gate, head_g, w_out, need_ctx):
    b, s, d = xl.shape
    heads = ML_HEADS
    nv = w_out.shape[0]
    dv = nv // heads
    nqk = (w_in.shape[1] - 2 * nv) // 2
    dqk = nqk // heads
    n_in = w_in.shape[1]
    ngate = w_gate.shape[1]
    n_all = -(-(n_in + LANE) // MXU_WIDTH) * MXU_WIDTH
    n_pad = n_all - n_in - ngate
    w = jnp.concatenate([w_in, w_gate, jnp.zeros((d, n_pad), F32)], axis=1).astype(BF16)
    bias = jnp.concatenate([jnp.zeros((n_in,), F32), b_gate, jnp.zeros((n_pad,), F32)])
    cos, sin = _rope_tables(s, dqk)
    w_out_b = w_out.astype(BF16)

    def project(x, mod):
        z, gates = norm_proj(x, g, mod, w, bias=bias, side=n_in)
        return z, (gates, jnp.swapaxes(gates[:, :, :ngate], 1, 2))

    def scan(z, gt, state, reverse, rotary):
        sx = z.shape[1]
        return mlstm_scan(z, gt, cos[:sx], sin[:sx], state, reverse=reverse, rotary=rotary,
                          heads=heads, dqk=dqk, dv=dv, chunk=min(ML_CHUNK, sx))

    def readout(hf, hb, z, x, mod):
        tm = _ptile(x.shape[1])
        o_blk = (2 * nqk + nv) // nv
        specs = [_row_spec(tm, nv), _row_spec(tm, nv), _row_spec(tm, nv, o_blk),
                 pl.BlockSpec((1, nv), lambda bi, i: (0, 0))]
        pro = functools.partial(_mlstm_readout_prologue, heads=heads, dv=dv)
        return resid_proj(pro, [hf, hb, z, head_g.reshape(1, nv)], specs, w_out_b, x, mod, gate_row=2, tm=tm)

    zero = (jnp.zeros((b, heads, dqk, dv), F32), jnp.zeros((b, heads, 1, dqk), F32),
            jnp.zeros((b, heads, 1, LANE), F32))
    zc, gtc = project(xc, mod_c)
    zl, gtl = project(xl, mod_l)
    hc_f, st_f = scan(zc, gtc, zero, False, False)
    hc_b, st_b = scan(zc, gtc, zero, True, False)
    hl_f, _ = scan(zl, gtl, st_f, False, True)
    hl_b, _ = scan(zl, gtl, st_b, True, True)
    yl = readout(hl_f, hl_b, zl, xl, mod_l)
    yc = readout(hc_f, hc_b, zc, xc, mod_c) if need_ctx else None
    return yl, yc


def na_mixer(xl, xc, g, mod_l, mod_c, w_qkv, q_g, k_g, rpb, w_o, need_ctx):
    d = xl.shape[2]
    wb, wob = w_qkv.astype(BF16), w_o.astype(BF16)
    zl = norm_proj(xl, g, mod_l, wb)
    zc = norm_proj(xc, g, mod_c, wb)
    ol = na_attention(zl, zc, q_g, k_g, rpb, heads=NA_HEADS)

    def out(o, x, mod):
        tm = _ptile(x.shape[1])
        return resid_proj(_cast_prologue, [o], [_row_spec(tm, d)], wob, x, mod, gate_row=2, tm=tm)

    yl = out(ol, xl, mod_l)
    yc = out(ctx_attention(zc, q_g, k_g, heads=NA_HEADS), xc, mod_c) if need_ctx else None
    return yl, yc


def conformer_mixer(x, g, mod, w_pw1, dw, dw_b, ln_g, ln_b, w_pw2):
    b, s, d = x.shape
    width = dw.shape[0]
    halo = 16
    assert width // 2 <= halo
    tm = _ptile(s)
    a = norm_proj(x, g, mod, w_pw1.astype(BF16), glu=True)
    vec = lambda: pl.BlockSpec((1, d), lambda bi, i: (0, 0))
    specs = _halo_specs(tm, halo, s, d) + [pl.BlockSpec((width, d), lambda bi, i: (0, 0)),
                                           vec(), vec(), vec()]
    pro = functools.partial(_conformer_prologue, tm=tm, width=width, halo=halo)
    return resid_proj(pro, [a, a, a, dw, dw_b.reshape(1, d), ln_g.reshape(1, d), ln_b.reshape(1, d)],
                      specs, w_pw2.astype(BF16), x, mod, gate_row=2, tm=tm,
                      scratch=[pltpu.VMEM((tm + 2 * halo, d), F32), pltpu.VMEM((tm, d), F32),
                               pltpu.VMEM((4, _conv_rows(tm, width)[1], LANE), F32)])


def rglru_mixer(xl, xc, g, mod_l, mod_c, w_in, conv_w, conv_b, w_gate, b_gate, lam, w_out, need_ctx):
    b, s, d = xl.shape
    r = w_out.shape[0]
    wb, wob, wgb = w_in.astype(BF16), w_out.astype(BF16), w_gate.astype(BF16)
    zl = norm_proj(xl, g, mod_l, wb)
    zc = norm_proj(xc, g, mod_c, wb)

    def scan(z, h0, reverse):
        k = 2 if reverse else 0
        return lru_scan(z, h0, conv_w, conv_b, wgb[k], wgb[k + 1], b_gate[k], b_gate[k + 1],
                        lam[1 if reverse else 0], reverse=reverse, tm=min(LRU_TILE, z.shape[1]))

    h0 = jnp.zeros((b, r), F32)
    hc_f = scan(zc, h0, False)
    hc_b = scan(zc, h0, True)
    hl_f = scan(zl, hc_f[:, -1], False)
    hl_b = scan(zl, hc_b[:, 0], True)

    def out(hf, hb, z, x, mod):
        tm = _ptile(x.shape[1])
        specs = [_row_spec(tm, r), _row_spec(tm, r), _row_spec(tm, r, 0)]
        return resid_proj(_lru_out_prologue, [hf, hb, z], specs, wob, x, mod, gate_row=2, tm=tm)

    yl = out(hl_f, hl_b, zl, xl, mod_l)
    yc = out(hc_f, hc_b, zc, xc, mod_c) if need_ctx else None
    return yl, yc


def kernel(x, c, ctx, c_ctx, norm_mix, norm_ffn, ada_w, ada_b, ml_w_in, ml_w_gate, ml_b_gate, ml_head_g, ml_w_out, na_w_qkv, na_q_g, na_k_g, na_rpb, na_w_o, cv_w_pw1, cv_dw, cv_dw_b, cv_ln_g, cv_ln_b, cv_w_pw2, lr_w_in, lr_conv, lr_conv_b, lr_w_gate, lr_b_gate, lr_lambda, lr_w_out, ffn_w_gu, ffn_conv, ffn_w_down):
    b, s, d = x.shape
    depth = ada_w.shape[0]
    n_mixers = 4
    cc_rows = -(-(b + 1) // SUBLANE) * SUBLANE
    cc = jnp.concatenate([c, c_ctx[None, :], jnp.zeros((cc_rows - b - 1, d), F32)], axis=0)
    mods = ada_modulation(cc, ada_w, ada_b).reshape(depth, cc_rows, 6, d)
    mods = jnp.pad(mods, ((0, 0), (0, 0), (0, MOD_ROWS - 6), (0, 0)))

    xl, xc = x, ctx
    for i in range(depth):
        kind, j = i % n_mixers, i // n_mixers
        need_ctx = i < depth - 1
        mod_l, mod_c = mods[i, :b], mods[i, b:b + 1]
        g = norm_mix[i]
        if kind == 0:
            xl, xc_new = mlstm_mixer(xl, xc, g, mod_l, mod_c, ml_w_in[j], ml_w_gate[j], ml_b_gate[j],
                                     ml_head_g[j], ml_w_out[j], need_ctx)
        elif kind == 1:
            xl, xc_new = na_mixer(xl, xc, g, mod_l, mod_c, na_w_qkv[j], na_q_g[j], na_k_g[j],
                                  na_rpb[j], na_w_o[j], need_ctx)
        elif kind == 2:
            cv = (cv_w_pw1[j], cv_dw[j], cv_dw_b[j], cv_ln_g[j], cv_ln_b[j], cv_w_pw2[j])
            xl, xc_new = conformer_mixer(xl, g, mod_l, *cv), None
            if need_ctx:
                xc_new = conformer_mixer(xc, g, mod_c, *cv)
        else:
            xl, xc_new = rglru_mixer(xl, xc, g, mod_l, mod_c, lr_w_in[j], lr_conv[j], lr_conv_b[j],
                                     lr_w_gate[j], lr_b_gate[j], lr_lambda[j], lr_w_out[j], need_ctx)
        wgu, wdn = ffn_w_gu[i].astype(BF16), ffn_w_down[i].astype(BF16)
        fc = 512 if wdn.shape[0] % 512 == 0 else LANE
        xl = conv_ffn(xl, norm_ffn[i], mod_l, wgu, ffn_conv[i], wdn, tm=ROW_TILE, fc=fc)
        if need_ctx:
            xc = conv_ffn(xc_new, norm_ffn[i], mod_c, wgu, ffn_conv[i], wdn, tm=ROW_TILE, fc=fc)
    return xl
```
